```python
import math
import jax
import jax.numpy as jnp
from jax import lax
import numpy as np

D_MODEL = 2048
BATCH = 4
SEQ = 2048
DEPTH = 2
DEC_BATCH = 128
DEC_SEQ = 8
PAST_LEN = 2048
PAGE_SIZE = 128

N_EVEN = (DEPTH + 1) // 2
N_ODD = DEPTH // 2
W_LRU = D_MODEL // 2
H_LRU = 16
BLK_LRU = W_LRU // H_LRU
CONV_LRU = 4
LRU_C = 8.0
H_ATT = 8
DV = (D_MODEL // 2) // H_ATT
DH = DV // 2
W_ATT = H_ATT * DV
W_IN_EVEN = 3 * W_ATT + 2 * W_LRU
W_OUT_EVEN = W_ATT + W_LRU
W_SC = D_MODEL
SC_CONV = 3
D_FF = 4 * D_MODEL
ROPE_THETA = 10000.0
RMS_EPS = 1e-6
Q_BLOCK = 128
NEG_INF = -1e30

kernel_name = "hybrid_rglru_diffattn_shortconv_decode_step"


def rmsnorm(x, g):
    xf = x.astype(jnp.float32)
    y = xf * lax.rsqrt(jnp.mean(xf * xf, axis=-1, keepdims=True) + RMS_EPS)
    return (y * g.astype(jnp.float32)).astype(x.dtype)


def ada_modulate(c, w, b):
    m = jax.nn.silu(c) @ w + b
    shift, scale, gate = jnp.split(m[:, None, :], 3, axis=-1)
    return shift, scale, gate


def modulated_norm(x, g, shift, scale):
    return rmsnorm(x, g) * (1.0 + scale) + shift


def rope(x, pos):
    half = x.shape[-1] // 2
    inv = ROPE_THETA ** (-jnp.arange(half, dtype=jnp.float32) / half)
    ang = pos.astype(jnp.float32)[:, None] * inv[None, :]
    cos = jnp.cos(ang)[:, None, None, :]
    sin = jnp.sin(ang)[:, None, None, :]
    xf = x.astype(jnp.float32)
    x1, x2 = xf[..., :half], xf[..., half:]
    return jnp.concatenate([x1 * cos - x2 * sin, x2 * cos + x1 * sin], axis=-1).astype(x.dtype)


def causal_dwconv(u, buf, w, b=None):
    width = w.shape[0]
    t = u.shape[1]
    xp = jnp.concatenate([buf.astype(u.dtype), u], axis=1)
    out = xp[:, 0:t] * w[0]
    for j in range(1, width):
        out = out + xp[:, j:j + t] * w[j]
    if b is not None:
        out = out + b
    return out, xp[:, xp.shape[1] - (width - 1):]


def rglru(x, h0, wa, ba, wx, bx, lam):
    bsz, t, width = x.shape
    xb = x.reshape(bsz, t, H_LRU, BLK_LRU)
    r = jax.nn.sigmoid(jnp.einsum("bthi,hij->bthj", xb, wa) + ba).reshape(bsz, t, width)
    i = jax.nn.sigmoid(jnp.einsum("bthi,hij->bthj", xb, wx) + bx).reshape(bsz, t, width)
    log_a = -LRU_C * r.astype(jnp.float32) * jax.nn.softplus(-lam.astype(jnp.float32))
    a = jnp.exp(log_a)
    drive = jnp.sqrt(-jnp.expm1(2.0 * log_a)) * (i * x).astype(jnp.float32)
    drive = drive.at[:, 0].add(a[:, 0] * h0.astype(jnp.float32))

    def combine(left, right):
        a1, b1 = left
        a2, b2 = right
        return a1 * a2, a2 * b1 + b2

    _, h = lax.associative_scan(combine, (a, drive), axis=1)
    return h.astype(x.dtype), h[:, -1].astype(x.dtype)


def diff_weights(s, lam):
    p = jax.nn.softmax(s, axis=-1)
    return p[:, :, 0] - lam * p[:, :, 1]


def attn_prompt(q, k, v, lam):
    bsz, t = q.shape[:2]
    nb = t // Q_BLOCK
    qb = jnp.moveaxis(q.reshape(bsz, nb, Q_BLOCK, H_ATT, 2, DH), 1, 0)
    kpos = jnp.arange(t)
    scale = DH ** -0.5

    def block(args):
        qi, bi = args
        s = jnp.einsum("bqhnd,bkhnd->bhnqk", qi, k).astype(jnp.float32) * scale
        qpos = bi * Q_BLOCK + jnp.arange(Q_BLOCK)
        s = jnp.where(kpos[None, :] <= qpos[:, None], s, NEG_INF)
        w = diff_weights(s, lam).astype(v.dtype)
        return jnp.einsum("bhqk,bkhd->bqhd", w, v)

    o = lax.map(block, (qb, jnp.arange(nb)))
    return jnp.moveaxis(o, 0, 1).reshape(bsz, t, H_ATT, DV)


def attn_sample(q, k, v, k_past, v_past, lam):
    t = q.shape[1]
    p = k_past.shape[1]
    scale = DH ** -0.5
    s_past = jnp.einsum("bqhnd,bkhnd->bhnqk", q, k_past).astype(jnp.float32) * scale
    s_new = jnp.einsum("bqhnd,bkhnd->bhnqk", q, k).astype(jnp.float32) * scale
    causal = jnp.arange(t)[None, :] <= jnp.arange(t)[:, None]
    s_new = jnp.where(causal, s_new, NEG_INF)
    w = diff_weights(jnp.concatenate([s_past, s_new], axis=-1), lam).astype(v.dtype)
    return (jnp.einsum("bhqk,bkhd->bqhd", w[..., :p], v_past)
            + jnp.einsum("bhqk,bkhd->bqhd", w[..., p:], v))


def even_mixer(h, pos, lru_h0, lru_buf, kv_cache, e, lam_init, W):
    bsz, t, _ = h.shape
    proj = h @ W["w_in_even"][e]
    q, k, v, xl, gl = jnp.split(proj, [W_ATT, 2 * W_ATT, 3 * W_ATT, 3 * W_ATT + W_LRU], axis=-1)
    q = rope(rmsnorm(q.reshape(bsz, t, H_ATT, 2, DH), W["q_norm"][e]), pos)
    k = rope(rmsnorm(k.reshape(bsz, t, H_ATT, 2, DH), W["k_norm"][e]), pos)
    v = v.reshape(bsz, t, H_ATT, DV)
    f32 = jnp.float32
    lam = (jnp.exp(jnp.sum(W["lambda_q1"][e].astype(f32) * W["lambda_k1"][e].astype(f32)))
           - jnp.exp(jnp.sum(W["lambda_q2"][e].astype(f32) * W["lambda_k2"][e].astype(f32)))
           + lam_init)
    if kv_cache is None:
        o = attn_prompt(q, k, v, lam)
    else:
        cache_k, cache_v, page_table = kv_cache
        k_past = cache_k[e, page_table].reshape(bsz, -1, H_ATT, 2, DH)
        v_past = cache_v[e, page_table].reshape(bsz, -1, H_ATT, DV)
        o = attn_sample(q, k, v, k_past, v_past, lam)
    o = (rmsnorm(o, W["subln"][e]) * (1.0 - lam_init)).reshape(bsz, t, W_ATT)
    xc, new_buf = causal_dwconv(xl, lru_buf, W["lru_conv_w"][e], W["lru_conv_b"][e])
    hs, h_last = rglru(xc, lru_h0, W["lru_wa"][e], W["lru_ba"][e], W["lru_wx"][e],
                       W["lru_bx"][e], W["lru_lam"][e])
    yl = jax.nn.gelu(gl) * hs
    y = jnp.concatenate([o, yl], axis=-1) @ W["w_out_even"][e]
    return y, k.reshape(bsz, t, H_ATT, 2 * DH), v, h_last, new_buf


def odd_mixer(h, buf, w_in, conv_w, w_out):
    b_gate, c_gate, xt = jnp.split(h @ w_in, 3, axis=-1)
    conv, new_buf = causal_dwconv(c_gate * xt, buf, conv_w)
    return (b_gate * conv) @ w_out, new_buf


def sq_relu_mlp(h, w1, w2):
    return jnp.square(jax.nn.relu(h @ w1)) @ w2


def trunk(x, c, pos, lru_h, lru_conv, sconv, kv_cache, W):
    ks, vs, hs, cls, scs = [], [], [], [], []
    for layer in range(DEPTH):
        shift, scale, gate = ada_modulate(c, W["ada_mix_w"][layer], W["ada_mix_b"][layer])
        h = modulated_norm(x, W["norm_mix"][layer], shift, scale)
        if layer % 2 == 0:
            e = layer // 2
            lam_init = 0.8 - 0.6 * math.exp(-0.3 * layer)
            y, k_new, v_new, h_last, conv_buf = even_mixer(h, pos, lru_h[e], lru_conv[e], kv_cache, e, lam_init, W)
            ks.append(k_new)
            vs.append(v_new)
            hs.append(h_last)
            cls.append(conv_buf)
        else:
            o = layer // 2
            y, sc_buf = odd_mixer(h, sconv[o], W["w_in_odd"][o], W["sconv_w"][o], W["w_out_odd"][o])
            scs.append(sc_buf)
        x = x + gate * y
        shift, scale, gate = ada_modulate(c, W["ada_mlp_w"][layer], W["ada_mlp_b"][layer])
        h = modulated_norm(x, W["norm_mlp"][layer], shift, scale)
        x = x + gate * sq_relu_mlp(h, W["mlp_w1"][layer], W["mlp_w2"][layer])
    return x, jnp.stack(ks), jnp.stack(vs), jnp.stack(hs), jnp.stack(cls), jnp.stack(scs)


def setup_inputs(seed: int = 0) -> dict:
    key = jax.random.key(seed)
    keys = iter(jax.random.split(key, 48))
    f32 = jnp.float32

    def nrm(shape, scale=1.0):
        return jax.random.normal(next(keys), shape, f32) * scale

    def gain(shape):
        return 1.0 + nrm(shape, 0.02)

    n_pages = PAST_LEN // PAGE_SIZE
    n_used = DEC_BATCH * n_pages
    n_pool = (5 * n_used) // 4
    page_table = jax.random.permutation(next(keys), n_pool)[:n_used].reshape(DEC_BATCH, n_pages).astype(jnp.int32)
    a0 = jax.random.uniform(next(keys), (N_EVEN, W_LRU), f32, 0.9, 0.999)
    lru_lam = jnp.log(a0) - jnp.log1p(-a0)
    d_in = D_MODEL ** -0.5
    return {
        "x_prompt": nrm((BATCH, SEQ, D_MODEL)),
        "x_sample": nrm((DEC_BATCH, DEC_SEQ, D_MODEL)),
        "cache_k": nrm((N_EVEN, n_pool, PAGE_SIZE, H_ATT, 2 * DH)),
        "cache_v": nrm((N_EVEN, n_pool, PAGE_SIZE, H_ATT, DV)),
        "state_lru_h": nrm((N_EVEN, DEC_BATCH, W_LRU), 0.5),
        "state_lru_conv": nrm((N_EVEN, DEC_BATCH, CONV_LRU - 1, W_LRU)),
        "state_sconv": nrm((N_ODD, DEC_BATCH, SC_CONV - 1, W_SC)),
        "page_table": page_table,
        "c_prompt": nrm((BATCH, D_MODEL)),
        "c_sample": nrm((DEC_BATCH, D_MODEL)),
        "norm_mix": gain((DEPTH, D_MODEL)),
        "norm_mlp": gain((DEPTH, D_MODEL)),
        "ada_mix_w": nrm((DEPTH, D_MODEL, 3 * D_MODEL), 0.5 * d_in),
        "ada_mix_b": nrm((DEPTH, 3 * D_MODEL), 0.02),
        "ada_mlp_w": nrm((DEPTH, D_MODEL, 3 * D_MODEL), 0.5 * d_in),
        "ada_mlp_b": nrm((DEPTH, 3 * D_MODEL), 0.02),
        "mlp_w1": nrm((DEPTH, D_MODEL, D_FF), d_in),
        "mlp_w2": nrm((DEPTH, D_FF, D_MODEL), D_FF ** -0.5),
        "w_in_even": nrm((N_EVEN, D_MODEL, W_IN_EVEN), d_in),
        "w_out_even": nrm((N_EVEN, W_OUT_EVEN, D_MODEL), W_OUT_EVEN ** -0.5),
        "lru_conv_w": nrm((N_EVEN, CONV_LRU, W_LRU), CONV_LRU ** -0.5),
        "lru_conv_b": nrm((N_EVEN, W_LRU), 0.02),
        "lru_wa": nrm((N_EVEN, H_LRU, BLK_LRU, BLK_LRU), BLK_LRU ** -0.5),
        "lru_ba": nrm((N_EVEN, H_LRU, BLK_LRU), 0.02),
        "lru_wx": nrm((N_EVEN, H_LRU, BLK_LRU, BLK_LRU), BLK_LRU ** -0.5),
        "lru_bx": nrm((N_EVEN, H_LRU, BLK_LRU), 0.02),
        "lru_lam": lru_lam,
        "q_norm": gain((N_EVEN, DH)),
        "k_norm": gain((N_EVEN, DH)),
        "lambda_q1": nrm((N_EVEN, DH), 0.1),
        "lambda_k1": nrm((N_EVEN, DH), 0.1),
        "lambda_q2": nrm((N_EVEN, DH), 0.1),
        "lambda_k2": nrm((N_EVEN, DH), 0.1),
        "subln": gain((N_EVEN, DV)),
        "w_in_odd": nrm((N_ODD, D_MODEL, 3 * W_SC), d_in),
        "sconv_w": nrm((N_ODD, SC_CONV, W_SC), SC_CONV ** -0.5),
        "w_out_odd": nrm((N_ODD, W_SC, D_MODEL), W_SC ** -0.5),
    }


def reference(x_prompt, x_sample, cache_k, cache_v, state_lru_h, state_lru_conv, state_sconv, page_table,
              c_prompt, c_sample, norm_mix, norm_mlp, ada_mix_w, ada_mix_b, ada_mlp_w, ada_mlp_b,
              mlp_w1, mlp_w2, w_in_even, w_out_even, lru_conv_w, lru_conv_b, lru_wa, lru_ba, lru_wx, lru_bx,
              lru_lam, q_norm, k_norm, lambda_q1, lambda_k1, lambda_q2, lambda_k2, subln,
              w_in_odd, sconv_w, w_out_odd):
    W = {
        "norm_mix": norm_mix, "norm_mlp": norm_mlp,
        "ada_mix_w": ada_mix_w, "ada_mix_b": ada_mix_b,
        "ada_mlp_w": ada_mlp_w, "ada_mlp_b": ada_mlp_b,
        "mlp_w1": mlp_w1, "mlp_w2": mlp_w2,
        "w_in_even": w_in_even, "w_out_even": w_out_even,
        "lru_conv_w": lru_conv_w, "lru_conv_b": lru_conv_b,
        "lru_wa": lru_wa, "lru_ba": lru_ba, "lru_wx": lru_wx, "lru_bx": lru_bx, "lru_lam": lru_lam,
        "q_norm": q_norm, "k_norm": k_norm,
        "lambda_q1": lambda_q1, "lambda_k1": lambda_k1, "lambda_q2": lambda_q2, "lambda_k2": lambda_k2,
        "subln": subln,
        "w_in_odd": w_in_odd, "sconv_w": sconv_w, "w_out_odd": w_out_odd,
    }
    bsz, seq = x_prompt.shape[0], x_prompt.shape[1]
    dec_seq = x_sample.shape[1]
    past_len = page_table.shape[1] * cache_k.shape[2]
    dt = x_prompt.dtype
    h0_p = jnp.zeros((N_EVEN, bsz, W_LRU), dt)
    conv0_p = jnp.zeros((N_EVEN, bsz, CONV_LRU - 1, W_LRU), dt)
    sconv0_p = jnp.zeros((N_ODD, bsz, SC_CONV - 1, W_SC), dt)
    y_prompt, k_p, v_p, h_p, cl_p, sc_p = trunk(
        x_prompt, c_prompt, jnp.arange(seq), h0_p, conv0_p, sconv0_p, None, W)
    y_sample, k_s, v_s, h_s, cl_s, sc_s = trunk(
        x_sample, c_sample, past_len + jnp.arange(dec_seq), state_lru_h, state_lru_conv, state_sconv,
        (cache_k, cache_v, page_table), W)
    return (y_prompt, y_sample, k_p, v_p, k_s, v_s, h_p, h_s, cl_p, cl_s, sc_p, sc_s)
```

```python
import functools
import math

import jax
import jax.numpy as jnp
from jax import lax
from jax.experimental import pallas as pl
from jax.experimental.pallas import tpu as pltpu

D_MODEL = 2048
DEPTH = 2
PAGE_SIZE = 128
N_EVEN = (DEPTH + 1) // 2
N_ODD = DEPTH // 2
W_LRU = D_MODEL // 2
H_LRU = 16
BLK_LRU = W_LRU // H_LRU
CONV_LRU = 4
LRU_C = 8.0
H_ATT = 8
DV = (D_MODEL // 2) // H_ATT
DH = DV // 2
W_ATT = H_ATT * DV
W_SC = D_MODEL
SC_CONV = 3
D_FF = 4 * D_MODEL
ROPE_THETA = 10000.0
RMS_EPS = 1e-6
NEG_INF = -1e30

F32 = jnp.float32
BF16 = jnp.bfloat16

SUBLANES = 8
LANES = 128
MIB = 1 << 20
VMEM_LIMIT = 56 * MIB

TM_PROJ = 1024
TM_OUT = 512
TF_MLP = 512
TN_ODD = 512
TQ_ATT = 256
T_LRU = 256
GATE_CHUNK = 256


def _params(sem):
    return pltpu.CompilerParams(dimension_semantics=sem, vmem_limit_bytes=VMEM_LIMIT)


def _dot(a, b):
    return jnp.dot(a, b, preferred_element_type=F32)


def _dot_nt(a, b):
    return lax.dot_general(a, b, (((1,), (1,)), ((), ())), preferred_element_type=F32)


def _modnorm(x, g, shift, scale):
    ms = jnp.mean(x * x, axis=-1, keepdims=True)
    y = x * lax.rsqrt(ms + RMS_EPS)
    return (y * g) * (1.0 + scale) + shift


def _ada_kernel(c_ref, w_ref, b_ref, o_ref):
    c = c_ref[...]
    a = (c * jax.nn.sigmoid(c)).astype(BF16)
    o_ref[...] = _dot(a, w_ref[...].astype(BF16)) + b_ref[...]


def _ada(c_all, w, b):
    rows = c_all.shape[0]
    tn = 1024
    return pl.pallas_call(
        _ada_kernel,
        grid=(DEPTH, 3 * D_MODEL // tn),
        in_specs=[
            pl.BlockSpec((rows, D_MODEL), lambda l, j: (0, 0)),
            pl.BlockSpec((None, D_MODEL, tn), lambda l, j: (l, 0, j)),
            pl.BlockSpec((None, 1, tn), lambda l, j: (l, 0, j)),
        ],
        out_specs=pl.BlockSpec((None, rows, tn), lambda l, j: (l, 0, j)),
        out_shape=jax.ShapeDtypeStruct((DEPTH, rows, 3 * D_MODEL), F32),
        compiler_params=_params(("arbitrary", "arbitrary")),
        name="ada_modulation",
    )(c_all, w, b.reshape(DEPTH, 1, 3 * D_MODEL))


class _Group:
    def __init__(self, batch, seq, per_token_mod):
        self.batch = batch
        self.seq = seq
        self.rows = batch * seq
        self.per_token_mod = per_token_mod
        self.tm_out = TM_OUT // 2 if per_token_mod else TM_OUT

    def mod_spec(self, tm, part, row_axis=0):
        if self.per_token_mod:
            return pl.BlockSpec((tm, D_MODEL), lambda *g: (g[row_axis], part))
        tiles_per_batch = self.seq // tm
        return pl.BlockSpec((None, 1, D_MODEL), lambda *g: (g[row_axis] // tiles_per_batch, 0, part))


def _modnorm_kernel(x_ref, g_ref, shift_ref, scale_ref, h_ref):
    h_ref[...] = _modnorm(x_ref[...], g_ref[...], shift_ref[...], scale_ref[...]).astype(BF16)


def _first_norm(grp, x, g, mod):
    tm = grp.tm_out
    return pl.pallas_call(
        _modnorm_kernel,
        grid=(grp.rows // tm,),
        in_specs=[
            pl.BlockSpec((tm, D_MODEL), lambda i: (i, 0)),
            pl.BlockSpec((1, D_MODEL), lambda i: (0, 0)),
            grp.mod_spec(tm, 0),
            grp.mod_spec(tm, 1),
        ],
        out_specs=pl.BlockSpec((tm, D_MODEL), lambda i: (i, 0)),
        out_shape=jax.ShapeDtypeStruct((grp.rows, D_MODEL), BF16),
        compiler_params=_params(("arbitrary",)),
        name="first_norm",
    )(x, g, mod, mod)


def _qk_epilogue(acc, g, cos, sin):
    tm = acc.shape[0]
    lane = lax.broadcasted_iota(jnp.int32, (tm, LANES), 1)
    first_map = lane < DH
    first_half = (lane % DH) < (DH // 2)
    out = []
    for hd in range(H_ATT):
        xs = acc[:, hd * LANES:(hd + 1) * LANES]
        sq = xs * xs
        lo = jnp.sum(jnp.where(first_map, sq, 0.0), axis=-1, keepdims=True)
        hi = jnp.sum(jnp.where(first_map, 0.0, sq), axis=-1, keepdims=True)
        ms = jnp.where(first_map, lo, hi) * (1.0 / DH)
        y = xs * lax.rsqrt(ms + RMS_EPS) * g
        rot = jnp.where(first_half,
                        pltpu.roll(y, LANES - DH // 2, axis=1),
                        pltpu.roll(y, DH // 2, axis=1))
        out.append(y * cos + rot * sin)
    return out, first_map


def _q_kernel(h_ref, w_ref, g_ref, cos_ref, sin_ref, qf_ref, q1_ref, q2_ref):
    acc = _dot(h_ref[...], w_ref[...])
    slabs, first_map = _qk_epilogue(acc, g_ref[...], cos_ref[...], sin_ref[...])
    scale = DH ** -0.5
    for hd, y in enumerate(slabs):
        y = y * scale
        cols = slice(hd * LANES, (hd + 1) * LANES)
        qf_ref[:, cols] = y
        q1_ref[:, cols] = jnp.where(first_map, y, 0.0).astype(BF16)
        q2_ref[:, cols] = jnp.where(first_map, 0.0, y).astype(BF16)


def _k_kernel(h_ref, w_ref, g_ref, cos_ref, sin_ref, kf_ref, kb_ref):
    acc = _dot(h_ref[...], w_ref[...])
    slabs, _ = _qk_epilogue(acc, g_ref[...], cos_ref[...], sin_ref[...])
    for hd, y in enumerate(slabs):
        cols = slice(hd * LANES, (hd + 1) * LANES)
        kf_ref[:, cols] = y
        kb_ref[:, cols] = y.astype(BF16)


def _v_kernel(h_ref, w_ref, vf_ref, vb_ref):
    acc = _dot(h_ref[...], w_ref[...])
    vf_ref[...] = acc
    vb_ref[...] = acc.astype(BF16)


def _plain_kernel(h_ref, w_ref, o_ref):
    o_ref[...] = _dot(h_ref[...], w_ref[...])


def _rope_spec(grp, tm):
    if grp.per_token_mod:
        return pl.BlockSpec((tm, LANES), lambda i, j: (i, 0))
    tiles_per_batch = grp.seq // tm
    return pl.BlockSpec((tm, LANES), lambda i, j: (i % tiles_per_batch, 0))


def _proj_call(kernel, grp, h, w, col_block, n_col_blocks, extra, extra_specs, out_dtypes, name):
    tm = min(TM_PROJ, grp.rows)
    tn = W_ATT
    out_spec = pl.BlockSpec((tm, tn), lambda i, j: (i, j))
    return pl.pallas_call(
        kernel,
        grid=(grp.rows // tm, n_col_blocks),
        in_specs=[
            pl.BlockSpec((tm, D_MODEL), lambda i, j: (i, 0)),
            pl.BlockSpec((D_MODEL, tn), lambda i, j: (0, col_block + j)),
        ] + extra_specs,
        out_specs=[out_spec] * len(out_dtypes),
        out_shape=[jax.ShapeDtypeStruct((grp.rows, tn * n_col_blocks), dt) for dt in out_dtypes],
        compiler_params=_params(("arbitrary", "arbitrary")),
        name=name,
    )(h, w, *extra)


def _even_in_proj(grp, h, w, qg, kg, cos, sin):
    tm = min(TM_PROJ, grp.rows)
    vec = pl.BlockSpec((1, LANES), lambda i, j: (0, 0))
    rope = _rope_spec(grp, tm)
    qf, q1, q2 = _proj_call(_q_kernel, grp, h, w, 0, 1, (qg, cos, sin), [vec, rope, rope],
                            (F32, BF16, BF16), "q_proj")
    kf, kb = _proj_call(_k_kernel, grp, h, w, 1, 1, (kg, cos, sin), [vec, rope, rope],
                        (F32, BF16), "k_proj")
    vf, vb = _proj_call(_v_kernel, grp, h, w, 2, 1, (), [], (F32, BF16), "v_proj")
    (xg,) = _proj_call(_plain_kernel, grp, h, w, 3, 2, (), [], (F32,), "lru_proj")
    return qf, q1, q2, kf, kb, vf, vb, xg


def _diff_lambda(lq1_ref, lk1_ref, lq2_ref, lk2_ref, lam_init):
    a = jnp.sum(lq1_ref[...] * lk1_ref[...], axis=-1, keepdims=True)
    b = jnp.sum(lq2_ref[...] * lk2_ref[...], axis=-1, keepdims=True)
    return jnp.exp(a) - jnp.exp(b) + lam_init


def _sub_norm(o, g, lam_init):
    ms = jnp.mean(o * o, axis=-1, keepdims=True)
    return (o * lax.rsqrt(ms + RMS_EPS) * g) * (1.0 - lam_init)


def _attn_prompt_kernel(lq1_ref, lk1_ref, lq2_ref, lk2_ref, g_ref, q1_ref, q2_ref, k_ref, v_ref, o_ref,
                        *, tq, lam_init):
    qi = pl.program_id(2)
    q1 = q1_ref[...]
    q2 = q2_ref[...]

    def tile(j, carry, diagonal):
        start = pl.multiple_of(j * tq, tq)
        kb = k_ref[pl.ds(start, tq), :]
        vb = v_ref[pl.ds(start, tq), :]
        new = []
        for q, (m, l, acc) in zip((q1, q2), carry):
            s = _dot_nt(q, kb)
            if diagonal:
                row = lax.broadcasted_iota(jnp.int32, (tq, tq), 0)
                col = lax.broadcasted_iota(jnp.int32, (tq, tq), 1)
                s = jnp.where(col <= row, s, NEG_INF)
            m_new = jnp.maximum(m, jnp.max(s, axis=-1, keepdims=True))
            alpha = jnp.exp(m - m_new)
            p = jnp.exp(s - m_new)
            l = alpha * l + jnp.sum(p, axis=-1, keepdims=True)
            acc = alpha * acc + _dot(p.astype(BF16), vb)
            new.append((m_new, l, acc))
        return tuple(new)

    init_one = (jnp.full((tq, 1), NEG_INF, F32), jnp.zeros((tq, 1), F32), jnp.zeros((tq, DV), F32))
    carry = lax.fori_loop(0, qi, lambda j, c: tile(j, c, False), (init_one, init_one))
    (_, l1, a1), (_, l2, a2) = tile(qi, carry, True)
    lam = _diff_lambda(lq1_ref, lk1_ref, lq2_ref, lk2_ref, lam_init)
    o = a1 / l1 - lam * (a2 / l2)
    o_ref[...] = _sub_norm(o, g_ref[...], lam_init).astype(BF16)


def _attn_prompt(grp, q1, q2, kb, vb, lam_vecs, subln, lam_init):
    tq = TQ_ATT
    tiles = grp.seq // tq
    vec64 = pl.BlockSpec((1, DH), lambda b, h, i: (0, 0))
    q_spec = pl.BlockSpec((tq, LANES), lambda b, h, i: (b * tiles + i, h))
    kv_spec = pl.BlockSpec((grp.seq, LANES), lambda b, h, i: (b, h))
    return pl.pallas_call(
        functools.partial(_attn_prompt_kernel, tq=tq, lam_init=lam_init),
        grid=(grp.batch, H_ATT, tiles),
        in_specs=[vec64] * 4 + [pl.BlockSpec((1, DV), lambda b, h, i: (0, 0)), q_spec, q_spec, kv_spec, kv_spec],
        out_specs=q_spec,
        out_shape=jax.ShapeDtypeStruct((grp.rows, W_ATT), BF16),
        compiler_params=_params(("arbitrary", "arbitrary", "arbitrary")),
        name="attn_prompt",
    )(*lam_vecs, subln, q1, q2, kb, vb)


def _attn_sample_kernel(pt_ref, lq1_ref, lk1_ref, lq2_ref, lk2_ref, g_ref, q_ref, kn_ref, vn_ref, *rest,
                        n_pages, lam_init):
    k_refs = rest[:n_pages]
    v_refs = rest[n_pages:2 * n_pages]
    o_ref = rest[2 * n_pages]
    del pt_ref
    n_q = q_ref.shape[0]
    rows = n_q * H_ATT
    q = q_ref[...].reshape(rows, LANES)
    lane = lax.broadcasted_iota(jnp.int32, (rows, LANES), 1)
    qbd = jnp.concatenate([jnp.where(lane < DH, q, 0.0), jnp.where(lane < DH, 0.0, q)], axis=0).astype(BF16)

    def block(kp, vp, carry, causal):
        m, l, acc = carry
        n_col = kp.shape[0]
        s = _dot_nt(qbd, kp.astype(BF16))
        row = lax.broadcasted_iota(jnp.int32, (2 * rows, n_col), 0)
        col = lax.broadcasted_iota(jnp.int32, (2 * rows, n_col), 1)
        keep = (row % H_ATT) == (col % H_ATT)
        if causal:
            keep = keep & ((col // H_ATT) <= ((row % rows) // H_ATT))
        s = jnp.where(keep, s, NEG_INF)
        m_new = jnp.maximum(m, jnp.max(s, axis=-1, keepdims=True))
        alpha = jnp.exp(m - m_new)
        p = jnp.exp(s - m_new)
        l = alpha * l + jnp.sum(p, axis=-1, keepdims=True)
        acc = alpha * acc + _dot(p.astype(BF16), vp.astype(BF16))
        return m_new, l, acc

    carry = (jnp.full((2 * rows, 1), NEG_INF, F32), jnp.zeros((2 * rows, 1), F32), jnp.zeros((2 * rows, DV), F32))
    for k_ref, v_ref in zip(k_refs, v_refs):
        carry = block(k_ref[...].reshape(PAGE_SIZE * H_ATT, LANES),
                      v_ref[...].reshape(PAGE_SIZE * H_ATT, LANES), carry, False)
    _, l, acc = block(kn_ref[...].reshape(rows, LANES), vn_ref[...].reshape(rows, LANES), carry, True)
    lam = _diff_lambda(lq1_ref, lk1_ref, lq2_ref, lk2_ref, lam_init)
    o = acc / l
    o = o[:rows] - lam * o[rows:]
    o_ref[...] = _sub_norm(o, g_ref[...], lam_init).reshape(n_q, H_ATT, DV)


def _attn_sample(grp, e, qf, kf, vf, cache_k, cache_v, page_table, lam_vecs, subln, lam_init):
    n_pages = page_table.shape[1]
    tok = (grp.batch, grp.seq, H_ATT, LANES)
    vec64 = pl.BlockSpec((1, DH), lambda b, pt: (0, 0))
    new_spec = pl.BlockSpec((None, grp.seq, H_ATT, LANES), lambda b, pt: (b, 0, 0, 0))

    def page_spec(p):
        return pl.BlockSpec((None, None, PAGE_SIZE, H_ATT, LANES),
                            lambda b, pt: (e, pt[b * n_pages + p], 0, 0, 0))

    pages = [page_spec(p) for p in range(n_pages)]
    out = pl.pallas_call(
        functools.partial(_attn_sample_kernel, n_pages=n_pages, lam_init=lam_init),
        grid_spec=pltpu.PrefetchScalarGridSpec(
            num_scalar_prefetch=1,
            grid=(grp.batch,),
            in_specs=[vec64] * 4 + [pl.BlockSpec((1, DV), lambda b, pt: (0, 0)), new_spec, new_spec, new_spec]
            + pages + pages,
            out_specs=new_spec,
        ),
        out_shape=jax.ShapeDtypeStruct(tok, F32),
        compiler_params=_params(("arbitrary",)),
        name="attn_sample",
    )(page_table.reshape(-1), *lam_vecs, subln, qf.reshape(tok), kf.reshape(tok), vf.reshape(tok),
      *([cache_k] * n_pages), *([cache_v] * n_pages))
    return out.reshape(grp.rows, W_ATT)


def _shifted(u, hist, s, period):
    n = u.shape[0]
    row = lax.broadcasted_iota(jnp.int32, u.shape, 0)
    cur = pltpu.roll(u, s, axis=0)
    old = pltpu.roll(hist, (n - period + s) % n, axis=0)
    return jnp.where((row % period) < s, old, cur)


def _scan_rows(a, b, period):
    row = lax.broadcasted_iota(jnp.int32, a.shape, 0) % period
    s = 1
    while s < period:
        valid = row >= s
        b = jnp.where(valid, a * pltpu.roll(b, s, axis=0) + b, b)
        a = jnp.where(valid, a * pltpu.roll(a, s, axis=0), a)
        s *= 2
    return a, b


def _lru_core(xl, hist, gl, h0, cw_ref, cb_ref, wa_ref, ba_ref, wx_ref, bx_ref, lam_ref, period):
    xc = xl * cw_ref[CONV_LRU - 1:CONV_LRU, :] + cb_ref[...]
    for s in range(1, CONV_LRU):
        xc = xc + _shifted(xl, hist, s, period) * cw_ref[CONV_LRU - 1 - s:CONV_LRU - s, :]
    xb = xc.astype(BF16)
    r_parts, i_parts = [], []
    for c in range(W_LRU // GATE_CHUNK):
        cols = slice(c * GATE_CHUNK, (c + 1) * GATE_CHUNK)
        r_parts.append(_dot(xb[:, cols], wa_ref[c]))
        i_parts.append(_dot(xb[:, cols], wx_ref[c]))
    r = jax.nn.sigmoid(jnp.concatenate(r_parts, axis=1) + ba_ref[...])
    i = jax.nn.sigmoid(jnp.concatenate(i_parts, axis=1) + bx_ref[...])
    log_a = (-LRU_C) * r * jax.nn.softplus(-lam_ref[...])
    a = jnp.exp(log_a)
    drive = jnp.sqrt(1.0 - a * a) * (i * xc)
    a_cum, h_zero = _scan_rows(a, drive, period)
    h = a_cum * h0 + h_zero
    return jax.nn.gelu(gl) * h, h


def _lru_prompt_kernel(xl_ref, gl_ref, cw_ref, cb_ref, wa_ref, ba_ref, wx_ref, bx_ref, lam_ref,
                       y_ref, hlast_ref, hist_ref, h_ref):
    t = pl.program_id(1)
    rows = xl_ref.shape[0]

    @pl.when(t == 0)
    def _():
        hist_ref[...] = jnp.zeros_like(hist_ref)
        h_ref[...] = jnp.zeros_like(h_ref)

    xl = xl_ref[...]
    y, h = _lru_core(xl, hist_ref[...], gl_ref[...], h_ref[0:1, :], cw_ref, cb_ref, wa_ref, ba_ref,
                     wx_ref, bx_ref, lam_ref, rows)
    y_ref[...] = y.astype(BF16)
    hist_ref[...] = xl
    last = h[rows - SUBLANES:, :]
    h_ref[...] = jnp.broadcast_to(last[SUBLANES - 1:, :], h_ref.shape)
    hlast_ref[...] = last


def _lru_sample_kernel(xl_ref, gl_ref, hist_ref, h0_ref, cw_ref, cb_ref, wa_ref, ba_ref, wx_ref, bx_ref, lam_ref,
                       y_ref, h_ref, *, period):
    y, h = _lru_core(xl_ref[...], hist_ref[...], gl_ref[...], h0_ref[...], cw_ref, cb_ref, wa_ref, ba_ref,
                     wx_ref, bx_ref, lam_ref, period)
    y_ref[...] = y.astype(BF16)
    h_ref[...] = h


def _lru_weight_specs(n_grid):
    zeros = (0,) * n_grid

    def full(shape):
        return pl.BlockSpec(shape, lambda *g: (0,) * len(shape))

    n_chunks = W_LRU // GATE_CHUNK
    return [full((CONV_LRU, W_LRU)), full((1, W_LRU)), full((n_chunks, GATE_CHUNK, GATE_CHUNK)), full((1, W_LRU)),
            full((n_chunks, GATE_CHUNK, GATE_CHUNK)), full((1, W_LRU)), full((1, W_LRU))]


def _lru_prompt(grp, xg, lru_w):
    tt = T_LRU
    tiles = grp.seq // tt
    y, hlast = pl.pallas_call(
        _lru_prompt_kernel,
        grid=(grp.batch, tiles),
        in_specs=[pl.BlockSpec((tt, W_LRU), lambda b, t: (b * tiles + t, 0)),
                  pl.BlockSpec((tt, W_LRU), lambda b, t: (b * tiles + t, 1))] + _lru_weight_specs(2),
        out_specs=[pl.BlockSpec((tt, W_LRU), lambda b, t: (b * tiles + t, 0)),
                   pl.BlockSpec((None, SUBLANES, W_LRU), lambda b, t: (b, 0, 0))],
        out_shape=[jax.ShapeDtypeStruct((grp.rows, W_LRU), BF16),
                   jax.ShapeDtypeStruct((grp.batch, SUBLANES, W_LRU), F32)],
        scratch_shapes=[pltpu.VMEM((tt, W_LRU), F32), pltpu.VMEM((SUBLANES, W_LRU), F32)],
        compiler_params=_params(("arbitrary", "arbitrary")),
        name="lru_prompt",
    )(xg, xg, *lru_w)
    return y, hlast[:, SUBLANES - 1, :]


def _lru_sample(grp, xg, hist, h0_rows, lru_w):
    tt = 256
    row = pl.BlockSpec((tt, W_LRU), lambda i: (i, 0))
    y, h = pl.pallas_call(
        functools.partial(_lru_sample_kernel, period=grp.seq),
        grid=(grp.rows // tt,),
        in_specs=[row, pl.BlockSpec((tt, W_LRU), lambda i: (i, 1)), row, row] + _lru_weight_specs(1),
        out_specs=[row, row],
        out_shape=[jax.ShapeDtypeStruct((grp.rows, W_LRU), BF16), jax.ShapeDtypeStruct((grp.rows, W_LRU), F32)],
        compiler_params=_params(("arbitrary",)),
        name="lru_sample",
    )(xg, xg, hist, h0_rows, *lru_w)
    return y, h.reshape(grp.batch, grp.seq, W_LRU)[:, grp.seq - 1, :]


def _residual_epilogue(acc, x_ref, gate_ref, norm, x_out_ref, h_out_ref):
    x_new = x_ref[...] + gate_ref[...] * acc
    x_out_ref[...] = x_new
    if h_out_ref is not None:
        g_ref, shift_ref, scale_ref = norm
        h_out_ref[...] = _modnorm(x_new, g_ref[...], shift_ref[...], scale_ref[...]).astype(BF16)


def _out_proj_kernel(*refs, n_in):
    a_refs = refs[:n_in]
    w_ref, x_ref, gate_ref, g_ref, shift_ref, scale_ref, x_out_ref, h_out_ref = refs[n_in:]
    acc = None
    k0 = 0
    for a_ref in a_refs:
        k = a_ref.shape[1]
        part = _dot(a_ref[...].astype(BF16), w_ref[k0:k0 + k, :])
        acc = part if acc is None else acc + part
        k0 += k
    _residual_epilogue(acc, x_ref, gate_ref, (g_ref, shift_ref, scale_ref), x_out_ref, h_out_ref)


def _out_proj(grp, acts, w, x, gate_mod, g_next, next_mod, name):
    tm = grp.tm_out
    row = pl.BlockSpec((tm, D_MODEL), lambda i: (i, 0))
    return pl.pallas_call(
        functools.partial(_out_proj_kernel, n_in=len(acts)),
        grid=(grp.rows // tm,),
        in_specs=[pl.BlockSpec((tm, a.shape[1]), lambda i: (i, 0)) for a in acts] + [
            pl.BlockSpec((D_MODEL, D_MODEL), lambda i: (0, 0)),
            row,
            grp.mod_spec(tm, 2),
            pl.BlockSpec((1, D_MODEL), lambda i: (0, 0)),
            grp.mod_spec(tm, 0),
            grp.mod_spec(tm, 1),
        ],
        out_specs=[row, row],
        out_shape=[jax.ShapeDtypeStruct((grp.rows, D_MODEL), F32), jax.ShapeDtypeStruct((grp.rows, D_MODEL), BF16)],
        compiler_params=_params(("arbitrary",)),
        name=name,
    )(*acts, w, x, gate_mod, g_next, next_mod, next_mod)


def _mlp_kernel(*refs, with_norm):
    if with_norm:
        h_ref, w1_ref, w2_ref, x_ref, gate_ref, g_ref, shift_ref, scale_ref, x_out_ref, h_out_ref, acc_ref = refs
        norm = (g_ref, shift_ref, scale_ref)
    else:
        h_ref, w1_ref, w2_ref, x_ref, gate_ref, x_out_ref, acc_ref = refs
        norm, h_out_ref = None, None
    f = pl.program_id(1)
    hid = jnp.maximum(_dot(h_ref[...], w1_ref[...]), 0.0)
    part = _dot((hid * hid).astype(BF16), w2_ref[...])

    @pl.when(f == 0)
    def _():
        acc_ref[...] = part

    @pl.when(f > 0)
    def _():
        acc_ref[...] += part

    @pl.when(f == pl.num_programs(1) - 1)
    def _():
        _residual_epilogue(acc_ref[...], x_ref, gate_ref, norm, x_out_ref, h_out_ref)


def _mlp(grp, h, w1, w2, x, gate_mod, g_next, next_mod, name):
    tm, tf = grp.tm_out, TF_MLP
    with_norm = g_next is not None
    row = pl.BlockSpec((tm, D_MODEL), lambda i, f: (i, 0))
    in_specs = [row,
                pl.BlockSpec((D_MODEL, tf), lambda i, f: (0, f)),
                pl.BlockSpec((tf, D_MODEL), lambda i, f: (f, 0)),
                row,
                grp.mod_spec(tm, 2)]
    args = [h, w1, w2, x, gate_mod]
    out_specs = [row]
    out_shape = [jax.ShapeDtypeStruct((grp.rows, D_MODEL), F32)]
    if with_norm:
        in_specs += [pl.BlockSpec((1, D_MODEL), lambda i, f: (0, 0)), grp.mod_spec(tm, 0), grp.mod_spec(tm, 1)]
        args += [g_next, next_mod, next_mod]
        out_specs.append(row)
        out_shape.append(jax.ShapeDtypeStruct((grp.rows, D_MODEL), BF16))
    return pl.pallas_call(
        functools.partial(_mlp_kernel, with_norm=with_norm),
        grid=(grp.rows // tm, D_FF // tf),
        in_specs=in_specs,
        out_specs=out_specs,
        out_shape=out_shape,
        scratch_shapes=[pltpu.VMEM((tm, D_MODEL), F32)],
        compiler_params=_params(("arbitrary", "arbitrary")),
        name=name,
    )(*args)


def _sconv_core(h_ref, wb_ref, wc_ref, wx_ref, cw_ref, hist, period):
    h = h_ref[...]
    u = _dot(h, wc_ref[...]) * _dot(h, wx_ref[...])
    conv = u * cw_ref[SC_CONV - 1:SC_CONV, :]
    for s in range(1, SC_CONV):
        conv = conv + _shifted(u, hist, s, period) * cw_ref[SC_CONV - 1 - s:SC_CONV - s, :]
    return u, (_dot(h, wb_ref[...]) * conv).astype(BF16)


def _sconv_prompt_kernel(h_ref, wb_ref, wc_ref, wx_ref, cw_ref, g_ref, tail_ref, hist_ref, *, tiles_per_batch):
    i = pl.program_id(1)
    rows = h_ref.shape[0]

    @pl.when(i % tiles_per_batch == 0)
    def _():
        hist_ref[...] = jnp.zeros_like(hist_ref)

    u, g = _sconv_core(h_ref, wb_ref, wc_ref, wx_ref, cw_ref, hist_ref[...], rows)
    g_ref[...] = g
    hist_ref[...] = u
    tail_ref[...] = u[rows - SUBLANES:, :]


def _sconv_sample_kernel(h_ref, wb_ref, wc_ref, wx_ref, cw_ref, hist_ref, g_ref, u_ref, *, period):
    u, g = _sconv_core(h_ref, wb_ref, wc_ref, wx_ref, cw_ref, hist_ref[...], period)
    g_ref[...] = g
    u_ref[...] = u


def _sconv_specs(tm, tn):
    n_blocks = W_SC // tn
    return [pl.BlockSpec((tm, D_MODEL), lambda n, i: (i, 0)),
            pl.BlockSpec((D_MODEL, tn), lambda n, i: (0, n)),
            pl.BlockSpec((D_MODEL, tn), lambda n, i: (0, n_blocks + n)),
            pl.BlockSpec((D_MODEL, tn), lambda n, i: (0, 2 * n_blocks + n)),
            pl.BlockSpec((SC_CONV, tn), lambda n, i: (0, n))]


def _sconv_prompt(grp, h, w_in, conv_w):
    tm, tn = TM_PROJ, TN_ODD
    tiles_per_batch = grp.seq // tm
    g, tail = pl.pallas_call(
        functools.partial(_sconv_prompt_kernel, tiles_per_batch=tiles_per_batch),
        grid=(W_SC // tn, grp.rows // tm),
        in_specs=_sconv_specs(tm, tn),
        out_specs=[pl.BlockSpec((tm, tn), lambda n, i: (i, n)),
                   pl.BlockSpec((None, SUBLANES, tn), lambda n, i: (i // tiles_per_batch, 0, n))],
        out_shape=[jax.ShapeDtypeStruct((grp.rows, W_SC), BF16),
                   jax.ShapeDtypeStruct((grp.batch, SUBLANES, W_SC), F32)],
        scratch_shapes=[pltpu.VMEM((tm, tn), F32)],
        compiler_params=_params(("arbitrary", "arbitrary")),
        name="sconv_prompt",
    )(h, w_in, w_in, w_in, conv_w)
    return g, tail[:, SUBLANES - (SC_CONV - 1):, :]


def _sconv_sample(grp, h, w_in, conv_w, hist):
    tm, tn = grp.rows, TN_ODD
    tile = pl.BlockSpec((tm, tn), lambda n, i: (i, n))
    g, u = pl.pallas_call(
        functools.partial(_sconv_sample_kernel, period=grp.seq),
        grid=(W_SC // tn, grp.rows // tm),
        in_specs=_sconv_specs(tm, tn) + [tile],
        out_specs=[tile, tile],
        out_shape=[jax.ShapeDtypeStruct((grp.rows, W_SC), BF16), jax.ShapeDtypeStruct((grp.rows, W_SC), F32)],
        compiler_params=_params(("arbitrary", "arbitrary")),
        name="sconv_sample",
    )(h, w_in, w_in, w_in, conv_w, hist)
    return g, u.reshape(grp.batch, grp.seq, W_SC)[:, grp.seq - (SC_CONV - 1):, :]


def _rope_tables(pos):
    half = DH // 2
    inv = ROPE_THETA ** (-jnp.arange(half, dtype=F32) / half)
    ang = pos.astype(F32)[:, None] * inv[None, :]
    cos = jnp.cos(ang)
    sin = jnp.sin(ang)
    cos = jnp.concatenate([cos, cos, cos, cos], axis=-1)
    sin = jnp.concatenate([-sin, sin, -sin, sin], axis=-1)
    return cos, sin


def _block_diag(w):
    per = GATE_CHUNK // BLK_LRU
    w = w.reshape(W_LRU // GATE_CHUNK, per, BLK_LRU, BLK_LRU)
    eye = jnp.eye(per, dtype=w.dtype)
    out = jnp.einsum("cpij,pq->cpiqj", w, eye)
    return out.reshape(W_LRU // GATE_CHUNK, GATE_CHUNK, GATE_CHUNK).astype(BF16)


def _history_rows(state, seq):
    b, k, w = state.shape
    return jnp.pad(state, ((0, 0), (seq - k, 0), (0, 0))).reshape(b * seq, w)


def _trunk(grp, x, mods, pos, state, kv_cache, W):
    lru_h0, lru_hist, sconv_hist = state
    outs = {}
    h = _first_norm(grp, x, W["norm_mix"][0], mods["mix"][0])
    for layer in range(DEPTH):
        if layer % 2 == 0:
            e = layer // 2
            lam_init = 0.8 - 0.6 * math.exp(-0.3 * layer)
            cos, sin = _rope_tables(pos)
            if grp.per_token_mod:
                cos, sin = jnp.tile(cos, (grp.batch, 1)), jnp.tile(sin, (grp.batch, 1))
            qf, q1, q2, kf, kb, vf, vb, xg = _even_in_proj(grp, h, W["w_in_even"][e], W["q_norm"][e],
                                                           W["k_norm"][e], cos, sin)
            lam_vecs = [W[n][e][None, :] for n in ("lambda_q1", "lambda_k1", "lambda_q2", "lambda_k2")]
            subln = W["subln"][e][None, :]
            lru_w = W["lru"][e]
            if kv_cache is None:
                o = _attn_prompt(grp, q1, q2, kb, vb, lam_vecs, subln, lam_init)
                yl, h_last = _lru_prompt(grp, xg, lru_w)
            else:
                cache_k, cache_v, page_table = kv_cache
                o = _attn_sample(grp, e, qf, kf, vf, cache_k, cache_v, page_table, lam_vecs, subln, lam_init)
                yl, h_last = _lru_sample(grp, xg, _history_rows(lru_hist[e], grp.seq),
                                         jnp.repeat(lru_h0[e], grp.seq, axis=0), lru_w)
            xl = xg[:, :W_LRU].reshape(grp.batch, grp.seq, W_LRU)
            outs.setdefault("k", []).append(kf.reshape(grp.batch, grp.seq, H_ATT, 2 * DH))
            outs.setdefault("v", []).append(vf.reshape(grp.batch, grp.seq, H_ATT, DV))
            outs.setdefault("h", []).append(h_last)
            outs.setdefault("cl", []).append(xl[:, grp.seq - (CONV_LRU - 1):, :])
            acts, w_out = [o, yl], W["w_out_even"][e]
        else:
            o_idx = layer // 2
            if kv_cache is None:
                g, sc = _sconv_prompt(grp, h, W["w_in_odd"][o_idx], W["sconv_w"][o_idx])
            else:
                g, sc = _sconv_sample(grp, h, W["w_in_odd"][o_idx], W["sconv_w"][o_idx],
                                      _history_rows(sconv_hist[o_idx], grp.seq))
            outs.setdefault("sc", []).append(sc)
            acts, w_out = [g], W["w_out_odd"][o_idx]
        x, h = _out_proj(grp, acts, w_out, x, mods["mix"][layer], W["norm_mlp"][layer], mods["mlp"][layer],
                         "out_proj_%d" % layer)
        if layer + 1 < DEPTH:
            x, h = _mlp(grp, h, W["mlp_w1"][layer], W["mlp_w2"][layer], x, mods["mlp"][layer],
                        W["norm_mix"][layer + 1], mods["mix"][layer + 1], "mlp_%d" % layer)
        else:
            (x,) = _mlp(grp, h, W["mlp_w1"][layer], W["mlp_w2"][layer], x, mods["mlp"][layer], None, None,
                        "mlp_%d" % layer)
    return x, {k: jnp.stack(v) for k, v in outs.items()}


def kernel(x_prompt, x_sample, cache_k, cache_v, state_lru_h, state_lru_conv, state_sconv, page_table, c_prompt, c_sample, norm_mix, norm_mlp, ada_mix_w, ada_mix_b, ada_mlp_w, ada_mlp_b, mlp_w1, mlp_w2, w_in_even, w_out_even, lru_conv_w, lru_conv_b, lru_wa, lru_ba, lru_wx, lru_bx, lru_lam, q_norm, k_norm, lambda_q1, lambda_k1, lambda_q2, lambda_k2, subln, w_in_odd, sconv_w, w_out_odd):
    bsz, seq, _ = x_prompt.shape
    dec_b, dec_seq, _ = x_sample.shape
    past_len = page_table.shape[1] * cache_k.shape[2]
    prompt = _Group(bsz, seq, per_token_mod=False)
    sample = _Group(dec_b, dec_seq, per_token_mod=True)

    n_c = bsz + dec_b
    pad = (-n_c) % (2 * SUBLANES)
    c_all = jnp.pad(jnp.concatenate([c_prompt, c_sample], axis=0), ((0, pad), (0, 0)))
    mods_p, mods_s = {}, {}
    for kind, w, b in (("mix", ada_mix_w, ada_mix_b), ("mlp", ada_mlp_w, ada_mlp_b)):
        m = _ada(c_all, w, b)
        mods_p[kind] = [m[l, :bsz].reshape(bsz, 1, 3 * D_MODEL) for l in range(DEPTH)]
        mods_s[kind] = [jnp.repeat(m[l, bsz:n_c], dec_seq, axis=0) for l in range(DEPTH)]

    row = lambda a: a.reshape(a.shape[0], 1, a.shape[-1])
    W = {
        "norm_mix": row(norm_mix), "norm_mlp": row(norm_mlp),
        "mlp_w1": mlp_w1.astype(BF16), "mlp_w2": mlp_w2.astype(BF16),
        "w_in_even": w_in_even.astype(BF16), "w_out_even": w_out_even.astype(BF16),
        "w_in_odd": w_in_odd.astype(BF16), "w_out_odd": w_out_odd.astype(BF16),
        "sconv_w": sconv_w,
        "q_norm": jnp.tile(q_norm, (1, 2))[:, None, :], "k_norm": jnp.tile(k_norm, (1, 2))[:, None, :],
        "lambda_q1": lambda_q1, "lambda_k1": lambda_k1, "lambda_q2": lambda_q2, "lambda_k2": lambda_k2,
        "subln": subln,
        "lru": [(lru_conv_w[e], lru_conv_b[e][None, :], _block_diag(lru_wa[e]), lru_ba[e].reshape(1, W_LRU),
                 _block_diag(lru_wx[e]), lru_bx[e].reshape(1, W_LRU), lru_lam[e][None, :])
                for e in range(N_EVEN)],
    }

    y_p, o_p = _trunk(prompt, x_prompt.reshape(prompt.rows, D_MODEL), mods_p, jnp.arange(seq),
                      (None, None, None), None, W)
    y_s, o_s = _trunk(sample, x_sample.reshape(sample.rows, D_MODEL), mods_s, past_len + jnp.arange(dec_seq),
                      (state_lru_h, state_lru_conv, state_sconv), (cache_k, cache_v, page_table), W)
    return (y_p.reshape(x_prompt.shape), y_s.reshape(x_sample.shape),
            o_p["k"], o_p["v"], o_s["k"], o_s["v"], o_p["h"], o_s["h"],
            o_p["cl"], o_s["cl"], o_p["sc"], o_s["sc"])
```

```python
import functools
import math

import jax
import jax.numpy as jnp
from jax import lax
from jax.experimental import pallas as pl
from jax.experimental.pallas import tpu as pltpu

D_MODEL = 2048
DEPTH = 2
PAGE_SIZE = 128
N_EVEN = (DEPTH + 1) // 2
N_ODD = DEPTH // 2
W_LRU = D_MODEL // 2
H_LRU = 16
BLK_LRU = W_LRU // H_LRU
CONV_LRU = 4
LRU_C = 8.0
H_ATT = 8
DV = (D_MODEL // 2) // H_ATT
DH = DV // 2
W_ATT = H_ATT * DV
W_SC = D_MODEL
SC_CONV = 3
D_FF = 4 * D_MODEL
ROPE_THETA = 10000.0
RMS_EPS = 1e-6
NEG_INF = -1e30

F32 = jnp.float32
BF16 = jnp.bfloat16

SUBLANES = 8
LANES = 128
MIB = 1 << 20
VMEM_LIMIT = 56 * MIB

TM_PROJ = 1024
TM_OUT = 512
TF_MLP = 1024
TN_ODD = 512
TQ_ATT = 256
T_LRU = 256
GATE_CHUNK = 256
EPILOGUE_ROWS = 128


def _params(sem):
    return pltpu.CompilerParams(dimension_semantics=sem, vmem_limit_bytes=VMEM_LIMIT)


def _dot(a, b):
    return jnp.dot(a, b, preferred_element_type=F32)


def _dot_nt(a, b):
    return lax.dot_general(a, b, (((1,), (1,)), ((), ())), preferred_element_type=F32)


def _modnorm(x, g, shift, scale):
    ms = jnp.mean(x * x, axis=-1, keepdims=True)
    y = x * lax.rsqrt(ms + RMS_EPS)
    return (y * g) * (1.0 + scale) + shift


def _mod_slabs(n_rows, mod_rows):
    step = EPILOGUE_ROWS if mod_rows == 1 else mod_rows
    return [slice(r, r + step) for r in range(0, n_rows, step)]


def _ada_kernel(c_ref, w_ref, b_ref, o_ref):
    c = c_ref[...]
    a = (c * jax.nn.sigmoid(c)).astype(BF16)
    o_ref[...] = _dot(a, w_ref[...].astype(BF16)) + b_ref[...]


def _ada(c_all, w, b):
    rows = c_all.shape[0]
    tn = 1024
    return pl.pallas_call(
        _ada_kernel,
        grid=(DEPTH, 3 * D_MODEL // tn),
        in_specs=[
            pl.BlockSpec((rows, D_MODEL), lambda l, j: (0, 0)),
            pl.BlockSpec((None, D_MODEL, tn), lambda l, j: (l, 0, j)),
            pl.BlockSpec((None, 1, tn), lambda l, j: (l, 0, j)),
        ],
        out_specs=pl.BlockSpec((None, rows, tn), lambda l, j: (l, 0, j)),
        out_shape=jax.ShapeDtypeStruct((DEPTH, rows, 3 * D_MODEL), F32),
        compiler_params=_params(("arbitrary", "arbitrary")),
        name="ada_modulation",
    )(c_all, w, b.reshape(DEPTH, 1, 3 * D_MODEL))


class _Group:
    def __init__(self, batch, seq, time_major):
        self.batch = batch
        self.seq = seq
        self.rows = batch * seq
        self.time_major = time_major

    def mod_spec(self, tm, part, row_axis=0):
        if self.time_major:
            assert tm % self.batch == 0
            return pl.BlockSpec((self.batch, D_MODEL), lambda *g: (0, part))
        tiles_per_batch = self.seq // tm
        return pl.BlockSpec((None, 1, D_MODEL), lambda *g: (g[row_axis] // tiles_per_batch, 0, part))


def _modnorm_kernel(x_ref, g_ref, shift_ref, scale_ref, h_ref):
    for sl in _mod_slabs(x_ref.shape[0], shift_ref.shape[0]):
        h_ref[sl, :] = _modnorm(x_ref[sl, :], g_ref[...], shift_ref[...], scale_ref[...]).astype(BF16)


def _first_norm(grp, x, g, mod):
    tm = TM_OUT
    return pl.pallas_call(
        _modnorm_kernel,
        grid=(grp.rows // tm,),
        in_specs=[
            pl.BlockSpec((tm, D_MODEL), lambda i: (i, 0)),
            pl.BlockSpec((1, D_MODEL), lambda i: (0, 0)),
            grp.mod_spec(tm, 0),
            grp.mod_spec(tm, 1),
        ],
        out_specs=pl.BlockSpec((tm, D_MODEL), lambda i: (i, 0)),
        out_shape=jax.ShapeDtypeStruct((grp.rows, D_MODEL), BF16),
        compiler_params=_params(("arbitrary",)),
        name="first_norm",
    )(x, g, mod, mod)


def _qk_epilogue(acc, g, cos, sin):
    tm = acc.shape[0]
    lane = lax.broadcasted_iota(jnp.int32, (tm, LANES), 1)
    first_map = lane < DH
    first_half = (lane % DH) < (DH // 2)
    out = []
    for hd in range(H_ATT):
        xs = acc[:, hd * LANES:(hd + 1) * LANES]
        sq = xs * xs
        lo = jnp.sum(jnp.where(first_map, sq, 0.0), axis=-1, keepdims=True)
        hi = jnp.sum(jnp.where(first_map, 0.0, sq), axis=-1, keepdims=True)
        ms = jnp.where(first_map, lo, hi) * (1.0 / DH)
        y = xs * lax.rsqrt(ms + RMS_EPS) * g
        rot = jnp.where(first_half,
                        pltpu.roll(y, LANES - DH // 2, axis=1),
                        pltpu.roll(y, DH // 2, axis=1))
        out.append(y * cos + rot * sin)
    return out, first_map


def _q_prompt_kernel(h_ref, w_ref, g_ref, cos_ref, sin_ref, q1_ref, q2_ref):
    acc = _dot(h_ref[...], w_ref[...])
    slabs, first_map = _qk_epilogue(acc, g_ref[...], cos_ref[...], sin_ref[...])
    for hd, y in enumerate(slabs):
        y = y * (DH ** -0.5)
        cols = slice(hd * LANES, (hd + 1) * LANES)
        q1_ref[:, cols] = jnp.where(first_map, y, 0.0).astype(BF16)
        q2_ref[:, cols] = jnp.where(first_map, 0.0, y).astype(BF16)


def _k_prompt_kernel(h_ref, w_ref, g_ref, cos_ref, sin_ref, kf_ref, kb_ref):
    acc = _dot(h_ref[...], w_ref[...])
    slabs, _ = _qk_epilogue(acc, g_ref[...], cos_ref[...], sin_ref[...])
    for hd, y in enumerate(slabs):
        cols = slice(hd * LANES, (hd + 1) * LANES)
        kf_ref[:, cols] = y
        kb_ref[:, cols] = y.astype(BF16)


def _qk_sample_kernel(h_ref, w_ref, g_ref, cos_ref, sin_ref, o_ref, *, scale):
    acc = _dot(h_ref[...], w_ref[...])
    slabs, _ = _qk_epilogue(acc, g_ref[...], cos_ref[...], sin_ref[...])
    for hd, y in enumerate(slabs):
        o_ref[:, hd * LANES:(hd + 1) * LANES] = y * scale


def _v_prompt_kernel(h_ref, w_ref, vf_ref, vb_ref):
    acc = _dot(h_ref[...], w_ref[...])
    vf_ref[...] = acc
    vb_ref[...] = acc.astype(BF16)


def _plain_kernel(h_ref, w_ref, o_ref):
    o_ref[...] = _dot(h_ref[...], w_ref[...])


def _proj_call(kernel, grp, h, w, col_block, n_col_blocks, extra, extra_specs, out_dtypes, name):
    tm = min(TM_PROJ, grp.rows)
    tn = W_ATT
    out_spec = pl.BlockSpec((tm, tn), lambda i, j: (i, j))
    return pl.pallas_call(
        kernel,
        grid=(grp.rows // tm, n_col_blocks),
        in_specs=[
            pl.BlockSpec((tm, D_MODEL), lambda i, j: (i, 0)),
            pl.BlockSpec((D_MODEL, tn), lambda i, j: (0, col_block + j)),
        ] + extra_specs,
        out_specs=[out_spec] * len(out_dtypes),
        out_shape=[jax.ShapeDtypeStruct((grp.rows, tn * n_col_blocks), dt) for dt in out_dtypes],
        compiler_params=_params(("arbitrary", "arbitrary")),
        name=name,
    )(h, w, *extra)


def _even_in_proj(grp, h, w, qg, kg, cos, sin):
    tm = min(TM_PROJ, grp.rows)
    vec = pl.BlockSpec((1, LANES), lambda i, j: (0, 0))
    tiles = cos.shape[0] // tm
    rope = pl.BlockSpec((tm, LANES), lambda i, j: (i % tiles, 0))
    rope_specs = [vec, rope, rope]
    if grp.time_major:
        q_kernel = functools.partial(_qk_sample_kernel, scale=DH ** -0.5)
        k_kernel = functools.partial(_qk_sample_kernel, scale=1.0)
        q = _proj_call(q_kernel, grp, h, w, 0, 1, (qg, cos, sin), rope_specs, (F32,), "q_proj")
        (kf,) = _proj_call(k_kernel, grp, h, w, 1, 1, (kg, cos, sin), rope_specs, (F32,), "k_proj")
        (vf,) = _proj_call(_plain_kernel, grp, h, w, 2, 1, (), [], (F32,), "v_proj")
        kv = (kf, vf)
    else:
        q = _proj_call(_q_prompt_kernel, grp, h, w, 0, 1, (qg, cos, sin), rope_specs, (BF16, BF16), "q_proj")
        kf, kb = _proj_call(_k_prompt_kernel, grp, h, w, 1, 1, (kg, cos, sin), rope_specs, (F32, BF16), "k_proj")
        vf, vb = _proj_call(_v_prompt_kernel, grp, h, w, 2, 1, (), [], (F32, BF16), "v_proj")
        kv = (kb, vb)
    (xg,) = _proj_call(_plain_kernel, grp, h, w, 3, 2, (), [], (F32,), "lru_proj")
    return q, kv, kf, vf, xg


def _diff_lambda(lq1_ref, lk1_ref, lq2_ref, lk2_ref, lam_init):
    a = jnp.sum(lq1_ref[...] * lk1_ref[...], axis=-1, keepdims=True)
    b = jnp.sum(lq2_ref[...] * lk2_ref[...], axis=-1, keepdims=True)
    return jnp.exp(a) - jnp.exp(b) + lam_init


def _sub_norm(o, g, lam_init):
    ms = jnp.mean(o * o, axis=-1, keepdims=True)
    return (o * lax.rsqrt(ms + RMS_EPS) * g) * (1.0 - lam_init)


def _attn_prompt_kernel(lq1_ref, lk1_ref, lq2_ref, lk2_ref, g_ref, q1_ref, q2_ref, k_ref, v_ref, o_ref,
                        *, tq, n_tiles, lam_init):
    qi = pl.program_id(2)
    lam = _diff_lambda(lq1_ref, lk1_ref, lq2_ref, lk2_ref, lam_init)
    row = lax.broadcasted_iota(jnp.int32, (tq, tq), 0)
    col = lax.broadcasted_iota(jnp.int32, (tq, tq), 1)
    causal = col <= row

    def body(c):
        n_past = c * tq
        kd = k_ref[n_past:n_past + tq, :]
        vd = v_ref[n_past:n_past + tq, :]
        maps = []
        for q_ref in (q1_ref, q2_ref):
            q = q_ref[...]
            sd = jnp.where(causal, _dot_nt(q, kd), NEG_INF)
            m = jnp.max(sd, axis=-1, keepdims=True)
            if c > 0:
                sp = _dot_nt(q, k_ref[0:n_past, :])
                m = jnp.maximum(m, jnp.max(sp, axis=-1, keepdims=True))
            pd = jnp.exp(sd - m)
            l = jnp.sum(pd, axis=-1, keepdims=True)
            acc = _dot(pd.astype(BF16), vd)
            if c > 0:
                pp = jnp.exp(sp - m)
                l = l + jnp.sum(pp, axis=-1, keepdims=True)
                acc = acc + _dot(pp.astype(BF16), v_ref[0:n_past, :])
            maps.append(acc / l)
        o = maps[0] - lam * maps[1]
        o_ref[...] = _sub_norm(o, g_ref[...], lam_init).astype(BF16)

    for c in range(n_tiles):
        pl.when(qi == c)(functools.partial(body, c))


def _attn_prompt(grp, q1, q2, kb, vb, lam_vecs, subln, lam_init):
    tq = TQ_ATT
    tiles = grp.seq // tq
    vec64 = pl.BlockSpec((1, DH), lambda b, h, i: (0, 0))
    q_spec = pl.BlockSpec((tq, LANES), lambda b, h, i: (b * tiles + i, h))
    kv_spec = pl.BlockSpec((grp.seq, LANES), lambda b, h, i: (b, h))
    return pl.pallas_call(
        functools.partial(_attn_prompt_kernel, tq=tq, n_tiles=tiles, lam_init=lam_init),
        grid=(grp.batch, H_ATT, tiles),
        in_specs=[vec64] * 4 + [pl.BlockSpec((1, DV), lambda b, h, i: (0, 0)), q_spec, q_spec, kv_spec, kv_spec],
        out_specs=q_spec,
        out_shape=jax.ShapeDtypeStruct((grp.rows, W_ATT), BF16),
        compiler_params=_params(("arbitrary", "arbitrary", "arbitrary")),
        name="attn_prompt",
    )(*lam_vecs, subln, q1, q2, kb, vb)


def _attn_sample_kernel(pt_ref, lq1_ref, lk1_ref, lq2_ref, lk2_ref, g_ref, q_ref, kn_ref, vn_ref, *rest,
                        n_pages, lam_init):
    k_refs = rest[:n_pages]
    v_refs = rest[n_pages:2 * n_pages]
    o_ref = rest[2 * n_pages]
    del pt_ref
    n_q = q_ref.shape[0]
    lam = _diff_lambda(lq1_ref, lk1_ref, lq2_ref, lk2_ref, lam_init)
    g = g_ref[...]
    lane = lax.broadcasted_iota(jnp.int32, (n_q, LANES), 1)
    t_of_row = lax.broadcasted_iota(jnp.int32, (2 * n_q, 1), 0) % n_q
    for hd in range(H_ATT):
        q = q_ref[:, hd, :]
        qbd = jnp.concatenate([jnp.where(lane < DH, q, 0.0), jnp.where(lane < DH, 0.0, q)], axis=0)
        head_rows = pl.ds(hd, PAGE_SIZE, stride=H_ATT)
        kh = jnp.concatenate([r[head_rows, :].astype(BF16) for r in k_refs], axis=0)
        vh = jnp.concatenate([r[head_rows, :].astype(BF16) for r in v_refs], axis=0)
        s = _dot_nt(qbd.astype(BF16), kh)
        kn = kn_ref[:, hd, :]
        vn = vn_ref[:, hd, :]
        s_new = [jnp.where(t_of_row >= j, jnp.sum(qbd * kn[j:j + 1, :], axis=-1, keepdims=True), NEG_INF)
                 for j in range(n_q)]
        m = jnp.max(s, axis=-1, keepdims=True)
        for sj in s_new:
            m = jnp.maximum(m, sj)
        p = jnp.exp(s - m)
        l = jnp.sum(p, axis=-1, keepdims=True)
        acc = _dot(p.astype(BF16), vh)
        for j, sj in enumerate(s_new):
            pj = jnp.exp(sj - m)
            l = l + pj
            acc = acc + pj * vn[j:j + 1, :]
        o = acc / l
        o = o[:n_q] - lam * o[n_q:]
        o_ref[:, hd, :] = _sub_norm(o, g, lam_init)


def _attn_sample(grp, e, q, kf, vf, cache_k, cache_v, page_table, lam_vecs, subln, lam_init):
    n_pages = page_table.shape[1]
    tok = (grp.seq, grp.batch, H_ATT, LANES)
    vec64 = pl.BlockSpec((1, DH), lambda b, pt: (0, 0))
    new_spec = pl.BlockSpec((grp.seq, None, H_ATT, LANES), lambda b, pt: (0, b, 0, 0))

    def page_spec(p):
        return pl.BlockSpec((None, None, PAGE_SIZE * H_ATT, LANES),
                            lambda b, pt: (e, pt[b * n_pages + p], 0, 0))

    pages = [page_spec(p) for p in range(n_pages)]
    page_rows = cache_k.shape[:2] + (PAGE_SIZE * H_ATT, LANES)
    cache_k = cache_k.reshape(page_rows)
    cache_v = cache_v.reshape(page_rows)
    out = pl.pallas_call(
        functools.partial(_attn_sample_kernel, n_pages=n_pages, lam_init=lam_init),
        grid_spec=pltpu.PrefetchScalarGridSpec(
            num_scalar_prefetch=1,
            grid=(grp.batch,),
            in_specs=[vec64] * 4 + [pl.BlockSpec((1, DV), lambda b, pt: (0, 0)), new_spec, new_spec, new_spec]
            + pages + pages,
            out_specs=new_spec,
        ),
        out_shape=jax.ShapeDtypeStruct(tok, F32),
        compiler_params=_params(("arbitrary",)),
        name="attn_sample",
    )(page_table.reshape(-1), *lam_vecs, subln, q.reshape(tok), kf.reshape(tok), vf.reshape(tok),
      *([cache_k] * n_pages), *([cache_v] * n_pages))
    return out.reshape(grp.rows, W_ATT)


def _shifted(u, hist, s):
    row = lax.broadcasted_iota(jnp.int32, u.shape, 0)
    return jnp.where(row < s, pltpu.roll(hist, s, axis=0), pltpu.roll(u, s, axis=0))


def _scan_rows(a, b):
    n = a.shape[0]
    row = lax.broadcasted_iota(jnp.int32, a.shape, 0)
    s = 1
    while s < n:
        valid = row >= s
        b = jnp.where(valid, a * pltpu.roll(b, s, axis=0) + b, b)
        a = jnp.where(valid, a * pltpu.roll(a, s, axis=0), a)
        s *= 2
    return a, b


def _lru_gates(xc, wa, ba, wx, bx, lam):
    xb = xc.astype(BF16)
    r_parts, i_parts = [], []
    for c, (wa_c, wx_c) in enumerate(zip(wa, wx)):
        cols = slice(c * GATE_CHUNK, (c + 1) * GATE_CHUNK)
        r_parts.append(_dot(xb[:, cols], wa_c))
        i_parts.append(_dot(xb[:, cols], wx_c))
    cat = (lambda parts: parts[0] if len(parts) == 1 else jnp.concatenate(parts, axis=1))
    r = jax.nn.sigmoid(cat(r_parts) + ba)
    i = jax.nn.sigmoid(cat(i_parts) + bx)
    log_a = (-LRU_C) * r * jax.nn.softplus(-lam)
    a = jnp.exp(log_a)
    drive = jnp.sqrt(1.0 - a * a) * (i * xc)
    return a, drive


def _lru_prompt_kernel(xl_ref, gl_ref, cw_ref, cb_ref, wa_ref, ba_ref, wx_ref, bx_ref, lam_ref,
                       y_ref, hlast_ref, ctail_ref, hist_ref, h_ref):
    t = pl.program_id(1)
    rows = xl_ref.shape[0]

    @pl.when(t == 0)
    def _():
        hist_ref[...] = jnp.zeros_like(hist_ref)
        h_ref[...] = jnp.zeros_like(h_ref)

    xl = xl_ref[...]
    hist = hist_ref[...]
    xc = xl * cw_ref[CONV_LRU - 1:CONV_LRU, :] + cb_ref[...]
    for s in range(1, CONV_LRU):
        xc = xc + _shifted(xl, hist, s) * cw_ref[CONV_LRU - 1 - s:CONV_LRU - s, :]
    n_chunks = W_LRU // GATE_CHUNK
    a, drive = _lru_gates(xc, [wa_ref[c] for c in range(n_chunks)], ba_ref[...],
                          [wx_ref[c] for c in range(n_chunks)], bx_ref[...], lam_ref[...])
    a_cum, h_zero = _scan_rows(a, drive)
    h = a_cum * h_ref[0:1, :] + h_zero
    y_ref[...] = (jax.nn.gelu(gl_ref[...]) * h).astype(BF16)
    hist_ref[...] = xl
    last = h[rows - SUBLANES:, :]
    h_ref[...] = jnp.broadcast_to(last[SUBLANES - 1:, :], h_ref.shape)
    hlast_ref[...] = last
    ctail_ref[...] = xl[rows - SUBLANES:, :]


def _lru_prompt(grp, xg, lru_w):
    tt = T_LRU
    tiles = grp.seq // tt
    n_chunks = W_LRU // GATE_CHUNK

    def full(shape):
        return pl.BlockSpec(shape, lambda b, t: (0,) * len(shape))

    tail = pl.BlockSpec((None, SUBLANES, W_LRU), lambda b, t: (b, 0, 0))
    tail_shape = jax.ShapeDtypeStruct((grp.batch, SUBLANES, W_LRU), F32)
    y, hlast, ctail = pl.pallas_call(
        _lru_prompt_kernel,
        grid=(grp.batch, tiles),
        in_specs=[pl.BlockSpec((tt, W_LRU), lambda b, t: (b * tiles + t, 0)),
                  pl.BlockSpec((tt, W_LRU), lambda b, t: (b * tiles + t, 1)),
                  full((CONV_LRU, W_LRU)), full((1, W_LRU)), full((n_chunks, GATE_CHUNK, GATE_CHUNK)),
                  full((1, W_LRU)), full((n_chunks, GATE_CHUNK, GATE_CHUNK)), full((1, W_LRU)), full((1, W_LRU))],
        out_specs=[pl.BlockSpec((tt, W_LRU), lambda b, t: (b * tiles + t, 0)), tail, tail],
        out_shape=[jax.ShapeDtypeStruct((grp.rows, W_LRU), BF16), tail_shape, tail_shape],
        scratch_shapes=[pltpu.VMEM((tt, W_LRU), F32), pltpu.VMEM((SUBLANES, W_LRU), F32)],
        compiler_params=_params(("arbitrary", "arbitrary")),
        name="lru_prompt",
    )(xg, xg, *lru_w)
    return y, hlast[:, SUBLANES - 1, :], ctail[:, SUBLANES - (CONV_LRU - 1):, :]


def _lru_sample_kernel(xl_ref, gl_ref, hist_ref, h0_ref, cw_ref, cb_ref, wa_ref, ba_ref, wx_ref, bx_ref, lam_ref,
                       y_ref, h_ref, *, n_t):
    nb = h0_ref.shape[0]

    def slab(t):
        return slice(t * nb, (t + 1) * nb)

    def u(t):
        return xl_ref[slab(t), :] if t >= 0 else hist_ref[slab(CONV_LRU - 1 + t), :]

    xc = []
    for t in range(n_t):
        acc = u(t) * cw_ref[CONV_LRU - 1:CONV_LRU, :] + cb_ref[...]
        for s in range(1, CONV_LRU):
            acc = acc + u(t - s) * cw_ref[CONV_LRU - 1 - s:CONV_LRU - s, :]
        xc.append(acc)
    xc = jnp.concatenate(xc, axis=0)
    a, drive = _lru_gates(xc, [wa_ref[0]], ba_ref[...], [wx_ref[0]], bx_ref[...], lam_ref[...])
    h = h0_ref[...]
    for t in range(n_t):
        h = a[slab(t), :] * h + drive[slab(t), :]
        y_ref[slab(t), :] = (jax.nn.gelu(gl_ref[slab(t), :]) * h).astype(BF16)
    h_ref[...] = h


def _lru_sample(grp, xg, hist, h0, lru_w):
    gc = GATE_CHUNK
    n_chunks = W_LRU // gc
    vec = pl.BlockSpec((1, gc), lambda c: (0, c))
    gate_w = pl.BlockSpec((1, gc, gc), lambda c: (c, 0, 0))
    return pl.pallas_call(
        functools.partial(_lru_sample_kernel, n_t=grp.seq),
        grid=(n_chunks,),
        in_specs=[pl.BlockSpec((grp.rows, gc), lambda c: (0, c)),
                  pl.BlockSpec((grp.rows, gc), lambda c: (0, n_chunks + c)),
                  pl.BlockSpec((hist.shape[0], gc), lambda c: (0, c)),
                  pl.BlockSpec((grp.batch, gc), lambda c: (0, c)),
                  pl.BlockSpec((CONV_LRU, gc), lambda c: (0, c)), vec, gate_w, vec, gate_w, vec, vec],
        out_specs=[pl.BlockSpec((grp.rows, gc), lambda c: (0, c)), pl.BlockSpec((grp.batch, gc), lambda c: (0, c))],
        out_shape=[jax.ShapeDtypeStruct((grp.rows, W_LRU), BF16), jax.ShapeDtypeStruct((grp.batch, W_LRU), F32)],
        compiler_params=_params(("arbitrary",)),
        name="lru_sample",
    )(xg, xg, hist, h0, *lru_w)


def _residual_epilogue(acc, x_ref, gate_ref, norm, x_out_ref, h_out_ref):
    for sl in _mod_slabs(x_ref.shape[0], gate_ref.shape[0]):
        x_new = x_ref[sl, :] + gate_ref[...] * acc[sl, :]
        x_out_ref[sl, :] = x_new
        if h_out_ref is not None:
            g_ref, shift_ref, scale_ref = norm
            h_out_ref[sl, :] = _modnorm(x_new, g_ref[...], shift_ref[...], scale_ref[...]).astype(BF16)


def _out_proj_kernel(*refs, n_in):
    a_refs = refs[:n_in]
    w_ref, x_ref, gate_ref, g_ref, shift_ref, scale_ref, x_out_ref, h_out_ref = refs[n_in:]
    acc = None
    k0 = 0
    for a_ref in a_refs:
        k = a_ref.shape[1]
        part = _dot(a_ref[...].astype(BF16), w_ref[k0:k0 + k, :])
        acc = part if acc is None else acc + part
        k0 += k
    _residual_epilogue(acc, x_ref, gate_ref, (g_ref, shift_ref, scale_ref), x_out_ref, h_out_ref)


def _out_proj(grp, acts, w, x, gate_mod, g_next, next_mod, name):
    tm = TM_OUT
    row = pl.BlockSpec((tm, D_MODEL), lambda i: (i, 0))
    return pl.pallas_call(
        functools.partial(_out_proj_kernel, n_in=len(acts)),
        grid=(grp.rows // tm,),
        in_specs=[pl.BlockSpec((tm, a.shape[1]), lambda i: (i, 0)) for a in acts] + [
            pl.BlockSpec((D_MODEL, D_MODEL), lambda i: (0, 0)),
            row,
            grp.mod_spec(tm, 2),
            pl.BlockSpec((1, D_MODEL), lambda i: (0, 0)),
            grp.mod_spec(tm, 0),
            grp.mod_spec(tm, 1),
        ],
        out_specs=[row, row],
        out_shape=[jax.ShapeDtypeStruct((grp.rows, D_MODEL), F32), jax.ShapeDtypeStruct((grp.rows, D_MODEL), BF16)],
        compiler_params=_params(("arbitrary",)),
        name=name,
    )(*acts, w, x, gate_mod, g_next, next_mod, next_mod)


def _mlp_kernel(*refs, with_norm):
    if with_norm:
        h_ref, w1_ref, w2_ref, x_ref, gate_ref, g_ref, shift_ref, scale_ref, x_out_ref, h_out_ref = refs
        norm = (g_ref, shift_ref, scale_ref)
    else:
        h_ref, w1_ref, w2_ref, x_ref, gate_ref, x_out_ref = refs
        norm, h_out_ref = None, None
    f = pl.program_id(1)

    @pl.when(f == 0)
    def _():
        x_out_ref[...] = jnp.zeros_like(x_out_ref)

    hid = jnp.maximum(_dot(h_ref[...], w1_ref[...]), 0.0)
    x_out_ref[...] += _dot((hid * hid).astype(BF16), w2_ref[...])

    @pl.when(f == pl.num_programs(1) - 1)
    def _():
        _residual_epilogue(x_out_ref, x_ref, gate_ref, norm, x_out_ref, h_out_ref)


def _mlp(grp, h, w1, w2, x, gate_mod, g_next, next_mod, name):
    tm, tf = TM_OUT, TF_MLP
    with_norm = g_next is not None
    row = pl.BlockSpec((tm, D_MODEL), lambda i, f: (i, 0))
    in_specs = [row,
                pl.BlockSpec((D_MODEL, tf), lambda i, f: (0, f)),
                pl.BlockSpec((tf, D_MODEL), lambda i, f: (f, 0)),
                row,
                grp.mod_spec(tm, 2)]
    args = [h, w1, w2, x, gate_mod]
    out_specs = [row]
    out_shape = [jax.ShapeDtypeStruct((grp.rows, D_MODEL), F32)]
    if with_norm:
        in_specs += [pl.BlockSpec((1, D_MODEL), lambda i, f: (0, 0)), grp.mod_spec(tm, 0), grp.mod_spec(tm, 1)]
        args += [g_next, next_mod, next_mod]
        out_specs.append(row)
        out_shape.append(jax.ShapeDtypeStruct((grp.rows, D_MODEL), BF16))
    return pl.pallas_call(
        functools.partial(_mlp_kernel, with_norm=with_norm),
        grid=(grp.rows // tm, D_FF // tf),
        in_specs=in_specs,
        out_specs=out_specs,
        out_shape=out_shape,
        compiler_params=_params(("arbitrary", "arbitrary")),
        name=name,
    )(*args)


def _sconv_prompt_kernel(h_ref, wb_ref, wc_ref, wx_ref, cw_ref, g_ref, tail_ref, hist_ref, *, tiles_per_batch):
    i = pl.program_id(1)
    rows = h_ref.shape[0]

    @pl.when(i % tiles_per_batch == 0)
    def _():
        hist_ref[...] = jnp.zeros_like(hist_ref)

    h = h_ref[...]
    u = _dot(h, wc_ref[...]) * _dot(h, wx_ref[...])
    hist = hist_ref[...]
    conv = u * cw_ref[SC_CONV - 1:SC_CONV, :]
    for s in range(1, SC_CONV):
        conv = conv + _shifted(u, hist, s) * cw_ref[SC_CONV - 1 - s:SC_CONV - s, :]
    g_ref[...] = (_dot(h, wb_ref[...]) * conv).astype(BF16)
    hist_ref[...] = u
    tail_ref[...] = u[rows - SUBLANES:, :]


def _sconv_sample_kernel(h_ref, wb_ref, wc_ref, wx_ref, cw_ref, hist_ref, g_ref, tail_ref, *, n_t):
    h = h_ref[...]
    u = _dot(h, wc_ref[...]) * _dot(h, wx_ref[...])
    b = _dot(h, wb_ref[...])
    nb = h.shape[0] // n_t

    def slab(t):
        return slice(t * nb, (t + 1) * nb)

    def ut(t):
        return u[slab(t), :] if t >= 0 else hist_ref[slab(SC_CONV - 1 + t), :]

    for t in range(n_t):
        conv = ut(t) * cw_ref[SC_CONV - 1:SC_CONV, :]
        for s in range(1, SC_CONV):
            conv = conv + ut(t - s) * cw_ref[SC_CONV - 1 - s:SC_CONV - s, :]
        g_ref[slab(t), :] = (b[slab(t), :] * conv).astype(BF16)
    tail_ref[...] = u[(n_t - (SC_CONV - 1)) * nb:, :]


def _sconv_specs(tm, tn):
    n_blocks = W_SC // tn
    return [pl.BlockSpec((tm, D_MODEL), lambda n, i: (i, 0)),
            pl.BlockSpec((D_MODEL, tn), lambda n, i: (0, n)),
            pl.BlockSpec((D_MODEL, tn), lambda n, i: (0, n_blocks + n)),
            pl.BlockSpec((D_MODEL, tn), lambda n, i: (0, 2 * n_blocks + n)),
            pl.BlockSpec((SC_CONV, tn), lambda n, i: (0, n))]


def _sconv_prompt(grp, h, w_in, conv_w):
    tm, tn = TM_PROJ, TN_ODD
    tiles_per_batch = grp.seq // tm
    g, tail = pl.pallas_call(
        functools.partial(_sconv_prompt_kernel, tiles_per_batch=tiles_per_batch),
        grid=(W_SC // tn, grp.rows // tm),
        in_specs=_sconv_specs(tm, tn),
        out_specs=[pl.BlockSpec((tm, tn), lambda n, i: (i, n)),
                   pl.BlockSpec((None, SUBLANES, tn), lambda n, i: (i // tiles_per_batch, 0, n))],
        out_shape=[jax.ShapeDtypeStruct((grp.rows, W_SC), BF16),
                   jax.ShapeDtypeStruct((grp.batch, SUBLANES, W_SC), F32)],
        scratch_shapes=[pltpu.VMEM((tm, tn), F32)],
        compiler_params=_params(("arbitrary", "arbitrary")),
        name="sconv_prompt",
    )(h, w_in, w_in, w_in, conv_w)
    return g, tail[:, SUBLANES - (SC_CONV - 1):, :]


def _sconv_sample(grp, h, w_in, conv_w, hist):
    tm, tn = grp.rows, TN_ODD
    n_tail = (SC_CONV - 1) * grp.batch
    return pl.pallas_call(
        functools.partial(_sconv_sample_kernel, n_t=grp.seq),
        grid=(W_SC // tn, 1),
        in_specs=_sconv_specs(tm, tn) + [pl.BlockSpec((n_tail, tn), lambda n, i: (0, n))],
        out_specs=[pl.BlockSpec((tm, tn), lambda n, i: (0, n)), pl.BlockSpec((n_tail, tn), lambda n, i: (0, n))],
        out_shape=[jax.ShapeDtypeStruct((grp.rows, W_SC), BF16), jax.ShapeDtypeStruct((n_tail, W_SC), F32)],
        compiler_params=_params(("arbitrary", "arbitrary")),
        name="sconv_sample",
    )(h, w_in, w_in, w_in, conv_w, hist)


def _rope_tables(pos):
    half = DH // 2
    inv = ROPE_THETA ** (-jnp.arange(half, dtype=F32) / half)
    ang = pos.astype(F32)[:, None] * inv[None, :]
    cos = jnp.cos(ang)
    sin = jnp.sin(ang)
    cos = jnp.concatenate([cos, cos, cos, cos], axis=-1)
    sin = jnp.concatenate([-sin, sin, -sin, sin], axis=-1)
    return cos, sin


def _block_diag(w):
    per = GATE_CHUNK // BLK_LRU
    w = w.reshape(W_LRU // GATE_CHUNK, per, BLK_LRU, BLK_LRU)
    eye = jnp.eye(per, dtype=w.dtype)
    out = jnp.einsum("cpij,pq->cpiqj", w, eye)
    return out.reshape(W_LRU // GATE_CHUNK, GATE_CHUNK, GATE_CHUNK).astype(BF16)


def _to_time_major(a):
    a = jnp.swapaxes(a, 0, 1)
    return a.reshape((a.shape[0] * a.shape[1],) + a.shape[2:])


def _from_time_major(a, batch):
    return jnp.swapaxes(a.reshape((a.shape[0] // batch, batch) + a.shape[1:]), 0, 1)


def _stack(parts):
    return parts[0][None] if len(parts) == 1 else jnp.stack(parts)


def _trunk(grp, x, mods, pos, state, kv_cache, W):
    lru_h0, lru_hist, sconv_hist = state
    outs = {n: [] for n in ("k", "v", "h", "cl", "sc")}
    unrow = (lambda a: _from_time_major(a, grp.batch)) if grp.time_major else (
        lambda a: a.reshape((grp.batch, a.shape[0] // grp.batch) + a.shape[1:]))
    h = _first_norm(grp, x, W["norm_mix"][0], mods["mix"][0])
    for layer in range(DEPTH):
        if layer % 2 == 0:
            e = layer // 2
            lam_init = 0.8 - 0.6 * math.exp(-0.3 * layer)
            cos, sin = _rope_tables(pos)
            if grp.time_major:
                cos, sin = jnp.repeat(cos, grp.batch, axis=0), jnp.repeat(sin, grp.batch, axis=0)
            q, kv, kf, vf, xg = _even_in_proj(grp, h, W["w_in_even"][e], W["q_norm"][e], W["k_norm"][e], cos, sin)
            lam_vecs = [W[n][e][None, :] for n in ("lambda_q1", "lambda_k1", "lambda_q2", "lambda_k2")]
            subln = W["subln"][e][None, :]
            if kv_cache is None:
                o = _attn_prompt(grp, q[0], q[1], kv[0], kv[1], lam_vecs, subln, lam_init)
                yl, h_last, conv_tail = _lru_prompt(grp, xg, W["lru"][e])
            else:
                cache_k, cache_v, page_table = kv_cache
                o = _attn_sample(grp, e, q[0], kv[0], kv[1], cache_k, cache_v, page_table, lam_vecs, subln, lam_init)
                yl, h_last = _lru_sample(grp, xg, _to_time_major(lru_hist[e]), lru_h0[e], W["lru"][e])
                conv_tail = unrow(xg[(grp.seq - (CONV_LRU - 1)) * grp.batch:, :W_LRU])
            outs["k"].append(unrow(kf).reshape(grp.batch, grp.seq, H_ATT, 2 * DH))
            outs["v"].append(unrow(vf).reshape(grp.batch, grp.seq, H_ATT, DV))
            outs["h"].append(h_last)
            outs["cl"].append(conv_tail)
            acts, w_out = [o, yl], W["w_out_even"][e]
        else:
            o_idx = layer // 2
            if kv_cache is None:
                g, sc = _sconv_prompt(grp, h, W["w_in_odd"][o_idx], W["sconv_w"][o_idx])
            else:
                g, sc = _sconv_sample(grp, h, W["w_in_odd"][o_idx], W["sconv_w"][o_idx],
                                      _to_time_major(sconv_hist[o_idx]))
                sc = unrow(sc)
            outs["sc"].append(sc)
            acts, w_out = [g], W["w_out_odd"][o_idx]
        x, h = _out_proj(grp, acts, w_out, x, mods["mix"][layer], W["norm_mlp"][layer], mods["mlp"][layer],
                         "out_proj_%d" % layer)
        if layer + 1 < DEPTH:
            x, h = _mlp(grp, h, W["mlp_w1"][layer], W["mlp_w2"][layer], x, mods["mlp"][layer],
                        W["norm_mix"][layer + 1], mods["mix"][layer + 1], "mlp_%d" % layer)
        else:
            (x,) = _mlp(grp, h, W["mlp_w1"][layer], W["mlp_w2"][layer], x, mods["mlp"][layer], None, None,
                        "mlp_%d" % layer)
    return unrow(x), {k: _stack(v) for k, v in outs.items()}


def kernel(x_prompt, x_sample, cache_k, cache_v, state_lru_h, state_lru_conv, state_sconv, page_table, c_prompt, c_sample, norm_mix, norm_mlp, ada_mix_w, ada_mix_b, ada_mlp_w, ada_mlp_b, mlp_w1, mlp_w2, w_in_even, w_out_even, lru_conv_w, lru_conv_b, lru_wa, lru_ba, lru_wx, lru_bx, lru_lam, q_norm, k_norm, lambda_q1, lambda_k1, lambda_q2, lambda_k2, subln, w_in_odd, sconv_w, w_out_odd):
    bsz, seq, _ = x_prompt.shape
    dec_b, dec_seq, _ = x_sample.shape
    past_len = page_table.shape[1] * cache_k.shape[2]
    prompt = _Group(bsz, seq, time_major=False)
    sample = _Group(dec_b, dec_seq, time_major=True)

    n_c = bsz + dec_b
    pad = (-n_c) % (2 * SUBLANES)
    c_all = jnp.pad(jnp.concatenate([c_sample, c_prompt], axis=0), ((0, pad), (0, 0)))
    mods_p, mods_s = {}, {}
    for kind, w, b in (("mix", ada_mix_w, ada_mix_b), ("mlp", ada_mlp_w, ada_mlp_b)):
        m = _ada(c_all, w, b)
        mods_s[kind] = [m[l, :dec_b] for l in range(DEPTH)]
        mods_p[kind] = [m[l, dec_b:n_c].reshape(bsz, 1, 3 * D_MODEL) for l in range(DEPTH)]

    row = lambda a: a.reshape(a.shape[0], 1, a.shape[-1])
    W = {
        "norm_mix": row(norm_mix), "norm_mlp": row(norm_mlp),
        "mlp_w1": mlp_w1.astype(BF16), "mlp_w2": mlp_w2.astype(BF16),
        "w_in_even": w_in_even.astype(BF16), "w_out_even": w_out_even.astype(BF16),
        "w_in_odd": w_in_odd.astype(BF16), "w_out_odd": w_out_odd.astype(BF16),
        "sconv_w": sconv_w,
        "q_norm": jnp.tile(q_norm, (1, 2))[:, None, :], "k_norm": jnp.tile(k_norm, (1, 2))[:, None, :],
        "lambda_q1": lambda_q1, "lambda_k1": lambda_k1, "lambda_q2": lambda_q2, "lambda_k2": lambda_k2,
        "subln": subln,
        "lru": [(lru_conv_w[e], lru_conv_b[e][None, :], _block_diag(lru_wa[e]), lru_ba[e].reshape(1, W_LRU),
                 _block_diag(lru_wx[e]), lru_bx[e].reshape(1, W_LRU), lru_lam[e][None, :])
                for e in range(N_EVEN)],
    }

    y_p, o_p = _trunk(prompt, x_prompt.reshape(prompt.rows, D_MODEL), mods_p, jnp.arange(seq),
                      (None, None, None), None, W)
    y_s, o_s = _trunk(sample, _to_time_major(x_sample), mods_s, past_len + jnp.arange(dec_seq),
                      (state_lru_h, state_lru_conv, state_sconv), (cache_k, cache_v, page_table), W)
    return (y_p, y_s, o_p["k"], o_p["v"], o_s["k"], o_s["v"], o_p["h"], o_s["h"],
            o_p["cl"], o_s["cl"], o_p["sc"], o_s["sc"])
```

```python
import functools
import math

import jax
import jax.numpy as jnp
from jax import lax
from jax.experimental import pallas as pl
from jax.experimental.pallas import tpu as pltpu

D_MODEL = 2048
DEPTH = 2
PAGE_SIZE = 128
N_EVEN = (DEPTH + 1) // 2
N_ODD = DEPTH // 2
W_LRU = D_MODEL // 2
H_LRU = 16
BLK_LRU = W_LRU // H_LRU
CONV_LRU = 4
LRU_C = 8.0
H_ATT = 8
DV = (D_MODEL // 2) // H_ATT
DH = DV // 2
W_ATT = H_ATT * DV
W_SC = D_MODEL
SC_CONV = 3
D_FF = 4 * D_MODEL
ROPE_THETA = 10000.0
RMS_EPS = 1e-6
NEG_INF = -1e30

F32 = jnp.float32
BF16 = jnp.bfloat16

SUBLANES = 8
LANES = 128
MIB = 1 << 20
VMEM_LIMIT = 56 * MIB

TM_PROJ = 1024
TM_OUT = 512
TF_MLP = 1024
TN_ODD = 512
TQ_ATT = 256
HEADS_PER_STEP = 2
T_LRU = 256
GATE_CHUNK = 256
EPILOGUE_ROWS = 128


def _params(sem):
    return pltpu.CompilerParams(dimension_semantics=sem, vmem_limit_bytes=VMEM_LIMIT)


def _dot(a, b):
    return jnp.dot(a, b, preferred_element_type=F32)


def _dot_nt(a, b):
    return lax.dot_general(a, b, (((1,), (1,)), ((), ())), preferred_element_type=F32)


def _modnorm(x, g, shift, scale):
    ms = jnp.mean(x * x, axis=-1, keepdims=True)
    y = x * lax.rsqrt(ms + RMS_EPS)
    return (y * g) * (1.0 + scale) + shift


def _mod_slabs(n_rows, mod_rows):
    step = EPILOGUE_ROWS if mod_rows == 1 else mod_rows
    return [slice(r, r + step) for r in range(0, n_rows, step)]


def _ada_kernel(c_ref, w_ref, b_ref, o_ref):
    c = c_ref[...]
    a = (c * jax.nn.sigmoid(c)).astype(BF16)
    o_ref[...] = _dot(a, w_ref[...].astype(BF16)) + b_ref[...]


def _ada(c_all, w, b):
    rows = c_all.shape[0]
    tn = 1024
    return pl.pallas_call(
        _ada_kernel,
        grid=(DEPTH, 3 * D_MODEL // tn),
        in_specs=[
            pl.BlockSpec((rows, D_MODEL), lambda l, j: (0, 0)),
            pl.BlockSpec((None, D_MODEL, tn), lambda l, j: (l, 0, j)),
            pl.BlockSpec((None, 1, tn), lambda l, j: (l, 0, j)),
        ],
        out_specs=pl.BlockSpec((None, rows, tn), lambda l, j: (l, 0, j)),
        out_shape=jax.ShapeDtypeStruct((DEPTH, rows, 3 * D_MODEL), F32),
        compiler_params=_params(("arbitrary", "arbitrary")),
        name="ada_modulation",
    )(c_all, w, b.reshape(DEPTH, 1, 3 * D_MODEL))


class _Group:
    def __init__(self, batch, seq, time_major):
        self.batch = batch
        self.seq = seq
        self.rows = batch * seq
        self.time_major = time_major

    def mod_spec(self, tm, part, row_axis=0):
        if self.time_major:
            assert tm % self.batch == 0
            return pl.BlockSpec((self.batch, D_MODEL), lambda *g: (0, part))
        tiles_per_batch = self.seq // tm
        return pl.BlockSpec((None, 1, D_MODEL), lambda *g: (g[row_axis] // tiles_per_batch, 0, part))


def _modnorm_kernel(x_ref, g_ref, shift_ref, scale_ref, h_ref):
    for sl in _mod_slabs(x_ref.shape[0], shift_ref.shape[0]):
        h_ref[sl, :] = _modnorm(x_ref[sl, :], g_ref[...], shift_ref[...], scale_ref[...]).astype(BF16)


def _first_norm(grp, x, g, mod):
    tm = TM_OUT
    return pl.pallas_call(
        _modnorm_kernel,
        grid=(grp.rows // tm,),
        in_specs=[
            pl.BlockSpec((tm, D_MODEL), lambda i: (i, 0)),
            pl.BlockSpec((1, D_MODEL), lambda i: (0, 0)),
            grp.mod_spec(tm, 0),
            grp.mod_spec(tm, 1),
        ],
        out_specs=pl.BlockSpec((tm, D_MODEL), lambda i: (i, 0)),
        out_shape=jax.ShapeDtypeStruct((grp.rows, D_MODEL), BF16),
        compiler_params=_params(("arbitrary",)),
        name="first_norm",
    )(x, g, mod, mod)


def _qk_epilogue(acc, g, cos, sin):
    tm = acc.shape[0]
    lane = lax.broadcasted_iota(jnp.int32, (tm, LANES), 1)
    first_map = lane < DH
    first_half = (lane % DH) < (DH // 2)
    out = []
    for hd in range(H_ATT):
        xs = acc[:, hd * LANES:(hd + 1) * LANES]
        sq = xs * xs
        lo = jnp.sum(jnp.where(first_map, sq, 0.0), axis=-1, keepdims=True)
        hi = jnp.sum(jnp.where(first_map, 0.0, sq), axis=-1, keepdims=True)
        ms = jnp.where(first_map, lo, hi) * (1.0 / DH)
        y = xs * lax.rsqrt(ms + RMS_EPS) * g
        rot = jnp.where(first_half,
                        pltpu.roll(y, LANES - DH // 2, axis=1),
                        pltpu.roll(y, DH // 2, axis=1))
        out.append(y * cos + rot * sin)
    return out, first_map


def _q_prompt_kernel(h_ref, w_ref, g_ref, cos_ref, sin_ref, q1_ref, q2_ref):
    acc = _dot(h_ref[...], w_ref[...])
    slabs, first_map = _qk_epilogue(acc, g_ref[...], cos_ref[...], sin_ref[...])
    for hd, y in enumerate(slabs):
        y = y * (DH ** -0.5)
        cols = slice(hd * LANES, (hd + 1) * LANES)
        q1_ref[:, cols] = jnp.where(first_map, y, 0.0).astype(BF16)
        q2_ref[:, cols] = jnp.where(first_map, 0.0, y).astype(BF16)


def _k_prompt_kernel(h_ref, w_ref, g_ref, cos_ref, sin_ref, kf_ref, kb_ref):
    acc = _dot(h_ref[...], w_ref[...])
    slabs, _ = _qk_epilogue(acc, g_ref[...], cos_ref[...], sin_ref[...])
    for hd, y in enumerate(slabs):
        cols = slice(hd * LANES, (hd + 1) * LANES)
        kf_ref[:, cols] = y
        kb_ref[:, cols] = y.astype(BF16)


def _qk_sample_kernel(h_ref, w_ref, g_ref, cos_ref, sin_ref, o_ref, *, scale):
    acc = _dot(h_ref[...], w_ref[...])
    slabs, _ = _qk_epilogue(acc, g_ref[...], cos_ref[...], sin_ref[...])
    for hd, y in enumerate(slabs):
        o_ref[:, hd * LANES:(hd + 1) * LANES] = y * scale


def _v_prompt_kernel(h_ref, w_ref, vf_ref, vb_ref):
    acc = _dot(h_ref[...], w_ref[...])
    vf_ref[...] = acc
    vb_ref[...] = acc.astype(BF16)


def _plain_kernel(h_ref, w_ref, o_ref):
    o_ref[...] = _dot(h_ref[...], w_ref[...])


def _proj_call(kernel, grp, h, w, col_block, n_col_blocks, extra, extra_specs, out_dtypes, name):
    w, idx = w
    tm = min(TM_PROJ, grp.rows)
    tn = W_ATT
    out_spec = pl.BlockSpec((tm, tn), lambda i, j: (i, j))
    return pl.pallas_call(
        kernel,
        grid=(grp.rows // tm, n_col_blocks),
        in_specs=[
            pl.BlockSpec((tm, D_MODEL), lambda i, j: (i, 0)),
            pl.BlockSpec((None, D_MODEL, tn), lambda i, j: (idx, 0, col_block + j)),
        ] + extra_specs,
        out_specs=[out_spec] * len(out_dtypes),
        out_shape=[jax.ShapeDtypeStruct((grp.rows, tn * n_col_blocks), dt) for dt in out_dtypes],
        compiler_params=_params(("arbitrary", "arbitrary")),
        name=name,
    )(h, w, *extra)


def _even_in_proj(grp, h, w, qg, kg, cos, sin):
    tm = min(TM_PROJ, grp.rows)
    vec = pl.BlockSpec((1, LANES), lambda i, j: (0, 0))
    tiles = cos.shape[0] // tm
    rope = pl.BlockSpec((tm, LANES), lambda i, j: (i % tiles, 0))
    rope_specs = [vec, rope, rope]
    if grp.time_major:
        q_kernel = functools.partial(_qk_sample_kernel, scale=DH ** -0.5)
        k_kernel = functools.partial(_qk_sample_kernel, scale=1.0)
        q = _proj_call(q_kernel, grp, h, w, 0, 1, (qg, cos, sin), rope_specs, (F32,), "q_proj")
        (kf,) = _proj_call(k_kernel, grp, h, w, 1, 1, (kg, cos, sin), rope_specs, (F32,), "k_proj")
        (vf,) = _proj_call(_plain_kernel, grp, h, w, 2, 1, (), [], (F32,), "v_proj")
        kv = (kf, vf)
    else:
        q = _proj_call(_q_prompt_kernel, grp, h, w, 0, 1, (qg, cos, sin), rope_specs, (BF16, BF16), "q_proj")
        kf, kb = _proj_call(_k_prompt_kernel, grp, h, w, 1, 1, (kg, cos, sin), rope_specs, (F32, BF16), "k_proj")
        vf, vb = _proj_call(_v_prompt_kernel, grp, h, w, 2, 1, (), [], (F32, BF16), "v_proj")
        kv = (kb, vb)
    (xg,) = _proj_call(_plain_kernel, grp, h, w, 3, 2, (), [], (F32,), "lru_proj")
    return q, kv, kf, vf, xg


def _diff_lambda(lq1_ref, lk1_ref, lq2_ref, lk2_ref, lam_init):
    a = jnp.sum(lq1_ref[...] * lk1_ref[...], axis=-1, keepdims=True)
    b = jnp.sum(lq2_ref[...] * lk2_ref[...], axis=-1, keepdims=True)
    return jnp.exp(a) - jnp.exp(b) + lam_init


def _sub_norm(o, g, lam_init):
    ms = jnp.mean(o * o, axis=-1, keepdims=True)
    return (o * lax.rsqrt(ms + RMS_EPS) * g) * (1.0 - lam_init)


def _attn_prompt_kernel(lq1_ref, lk1_ref, lq2_ref, lk2_ref, g_ref, q1_ref, q2_ref, k_ref, v_ref, o_ref,
                        *, tq, n_tiles, lam_init):
    qi = pl.program_id(2)
    lam = _diff_lambda(lq1_ref, lk1_ref, lq2_ref, lk2_ref, lam_init)
    row = lax.broadcasted_iota(jnp.int32, (tq, tq), 0)
    col = lax.broadcasted_iota(jnp.int32, (tq, tq), 1)
    causal = col <= row

    def body(c):
        n_past = c * tq
        for hd in range(o_ref.shape[1] // DV):
            cols = slice(hd * DV, (hd + 1) * DV)
            kd = k_ref[n_past:n_past + tq, cols]
            vd = v_ref[n_past:n_past + tq, cols]
            maps = []
            for q_ref in (q1_ref, q2_ref):
                q = q_ref[:, cols]
                sd = jnp.where(causal, _dot_nt(q, kd), NEG_INF)
                m = jnp.max(sd, axis=-1, keepdims=True)
                if c > 0:
                    sp = _dot_nt(q, k_ref[0:n_past, cols])
                    m = jnp.maximum(m, jnp.max(sp, axis=-1, keepdims=True))
                pd = jnp.exp(sd - m)
                l = jnp.sum(pd, axis=-1, keepdims=True)
                acc = _dot(pd.astype(BF16), vd)
                if c > 0:
                    pp = jnp.exp(sp - m)
                    l = l + jnp.sum(pp, axis=-1, keepdims=True)
                    acc = acc + _dot(pp.astype(BF16), v_ref[0:n_past, cols])
                maps.append(acc / l)
            o = maps[0] - lam * maps[1]
            o_ref[:, cols] = _sub_norm(o, g_ref[...], lam_init).astype(BF16)

    for c in range(n_tiles):
        pl.when(qi == c)(functools.partial(body, c))


def _attn_prompt(grp, q1, q2, kb, vb, lam_vecs, subln, lam_init):
    tq = TQ_ATT
    tiles = grp.seq // tq
    width = HEADS_PER_STEP * DV
    vec64 = pl.BlockSpec((1, DH), lambda b, h, i: (0, 0))
    q_spec = pl.BlockSpec((tq, width), lambda b, h, i: (b * tiles + i, h))
    kv_spec = pl.BlockSpec((grp.seq, width), lambda b, h, i: (b, h))
    return pl.pallas_call(
        functools.partial(_attn_prompt_kernel, tq=tq, n_tiles=tiles, lam_init=lam_init),
        grid=(grp.batch, H_ATT // HEADS_PER_STEP, tiles),
        in_specs=[vec64] * 4 + [pl.BlockSpec((1, DV), lambda b, h, i: (0, 0)), q_spec, q_spec, kv_spec, kv_spec],
        out_specs=q_spec,
        out_shape=jax.ShapeDtypeStruct((grp.rows, W_ATT), BF16),
        compiler_params=_params(("arbitrary", "arbitrary", "arbitrary")),
        name="attn_prompt",
    )(*lam_vecs, subln, q1, q2, kb, vb)


def _attn_sample_kernel(pt_ref, lq1_ref, lk1_ref, lq2_ref, lk2_ref, g_ref, q_ref, kn_ref, vn_ref, *rest,
                        n_pages, lam_init):
    k_refs = rest[:n_pages]
    v_refs = rest[n_pages:2 * n_pages]
    o_ref = rest[2 * n_pages]
    del pt_ref
    n_q = q_ref.shape[0]
    half = H_ATT // 2
    per_head = 2 * n_q
    n_past = n_pages * PAGE_SIZE
    lam = _diff_lambda(lq1_ref, lk1_ref, lq2_ref, lk2_ref, lam_init)
    g = g_ref[...]
    lane = lax.broadcasted_iota(jnp.int32, (n_q, LANES), 1)
    row = lax.broadcasted_iota(jnp.int32, (2 * per_head, 1), 0)
    t_of_row = row % n_q
    second = row >= per_head
    col = lax.broadcasted_iota(jnp.int32, (2 * per_head, 2 * n_past), 1)
    own_head = (col % 2) == (lax.broadcasted_iota(jnp.int32, (2 * per_head, 2 * n_past), 0) // per_head)
    for hp in range(half):
        heads = (hp, hp + half)
        qbd = []
        for hd in heads:
            q = q_ref[:, hd, :]
            qbd += [jnp.where(lane < DH, q, 0.0), jnp.where(lane < DH, 0.0, q)]
        qbd = jnp.concatenate(qbd, axis=0)
        pair_rows = pl.ds(hp, 2 * PAGE_SIZE, stride=half)
        kp = jnp.concatenate([r[pair_rows, :].astype(BF16) for r in k_refs], axis=0)
        vp = jnp.concatenate([r[pair_rows, :].astype(BF16) for r in v_refs], axis=0)
        s = jnp.where(own_head, _dot_nt(qbd.astype(BF16), kp), NEG_INF)
        kn = [kn_ref[:, hd, :] for hd in heads]
        vn = [vn_ref[:, hd, :] for hd in heads]
        s_new = []
        for j in range(n_q):
            kj = jnp.where(second, kn[1][j:j + 1, :], kn[0][j:j + 1, :])
            s_new.append(jnp.where(t_of_row >= j, jnp.sum(qbd * kj, axis=-1, keepdims=True), NEG_INF))
        m = jnp.max(s, axis=-1, keepdims=True)
        for sj in s_new:
            m = jnp.maximum(m, sj)
        p = jnp.exp(s - m)
        l = jnp.sum(p, axis=-1, keepdims=True)
        acc = _dot(p.astype(BF16), vp)
        for j, sj in enumerate(s_new):
            pj = jnp.exp(sj - m)
            l = l + pj
            acc = acc + pj * jnp.where(second, vn[1][j:j + 1, :], vn[0][j:j + 1, :])
        o = acc / l
        for a, hd in enumerate(heads):
            oa = o[a * per_head:a * per_head + n_q] - lam * o[a * per_head + n_q:(a + 1) * per_head]
            o_ref[:, hd, :] = _sub_norm(oa, g, lam_init)


def _attn_sample(grp, e, q, kf, vf, cache_k, cache_v, page_table, lam_vecs, subln, lam_init):
    n_pages = page_table.shape[1]
    tok = (grp.seq, grp.batch, H_ATT, LANES)
    vec64 = pl.BlockSpec((1, DH), lambda b, pt: (0, 0))
    new_spec = pl.BlockSpec((grp.seq, None, H_ATT, LANES), lambda b, pt: (0, b, 0, 0))

    def page_spec(p):
        return pl.BlockSpec((None, None, PAGE_SIZE * H_ATT, LANES),
                            lambda b, pt: (e, pt[b * n_pages + p], 0, 0))

    pages = [page_spec(p) for p in range(n_pages)]
    page_rows = cache_k.shape[:2] + (PAGE_SIZE * H_ATT, LANES)
    cache_k = cache_k.reshape(page_rows)
    cache_v = cache_v.reshape(page_rows)
    out = pl.pallas_call(
        functools.partial(_attn_sample_kernel, n_pages=n_pages, lam_init=lam_init),
        grid_spec=pltpu.PrefetchScalarGridSpec(
            num_scalar_prefetch=1,
            grid=(grp.batch,),
            in_specs=[vec64] * 4 + [pl.BlockSpec((1, DV), lambda b, pt: (0, 0)), new_spec, new_spec, new_spec]
            + pages + pages,
            out_specs=new_spec,
        ),
        out_shape=jax.ShapeDtypeStruct(tok, F32),
        compiler_params=_params(("arbitrary",)),
        name="attn_sample",
    )(page_table.reshape(-1), *lam_vecs, subln, q.reshape(tok), kf.reshape(tok), vf.reshape(tok),
      *([cache_k] * n_pages), *([cache_v] * n_pages))
    return out.reshape(grp.rows, W_ATT)


def _shifted(u, hist, s):
    row = lax.broadcasted_iota(jnp.int32, u.shape, 0)
    return jnp.where(row < s, pltpu.roll(hist, s, axis=0), pltpu.roll(u, s, axis=0))


def _scan_rows(a, b):
    n = a.shape[0]
    row = lax.broadcasted_iota(jnp.int32, a.shape, 0)
    s = 1
    while s < n:
        valid = row >= s
        b = jnp.where(valid, a * pltpu.roll(b, s, axis=0) + b, b)
        a = jnp.where(valid, a * pltpu.roll(a, s, axis=0), a)
        s *= 2
    return a, b


def _lru_gates(xc, wa, ba, wx, bx, lam):
    xb = xc.astype(BF16)
    r_parts, i_parts = [], []
    for c, (wa_c, wx_c) in enumerate(zip(wa, wx)):
        cols = slice(c * GATE_CHUNK, (c + 1) * GATE_CHUNK)
        r_parts.append(_dot(xb[:, cols], wa_c))
        i_parts.append(_dot(xb[:, cols], wx_c))
    cat = (lambda parts: parts[0] if len(parts) == 1 else jnp.concatenate(parts, axis=1))
    r = jax.nn.sigmoid(cat(r_parts) + ba)
    i = jax.nn.sigmoid(cat(i_parts) + bx)
    log_a = (-LRU_C) * r * jax.nn.softplus(-lam)
    a = jnp.exp(log_a)
    drive = jnp.sqrt(1.0 - a * a) * (i * xc)
    return a, drive


def _lru_prompt_kernel(xl_ref, gl_ref, cw_ref, cb_ref, wa_ref, ba_ref, wx_ref, bx_ref, lam_ref,
                       y_ref, hlast_ref, ctail_ref, hist_ref, h_ref):
    t = pl.program_id(1)
    rows = xl_ref.shape[0]

    @pl.when(t == 0)
    def _():
        hist_ref[...] = jnp.zeros_like(hist_ref)
        h_ref[...] = jnp.zeros_like(h_ref)

    xl = xl_ref[...]
    hist = hist_ref[...]
    xc = xl * cw_ref[CONV_LRU - 1:CONV_LRU, :] + cb_ref[...]
    for s in range(1, CONV_LRU):
        xc = xc + _shifted(xl, hist, s) * cw_ref[CONV_LRU - 1 - s:CONV_LRU - s, :]
    n_chunks = W_LRU // GATE_CHUNK
    a, drive = _lru_gates(xc, [wa_ref[c] for c in range(n_chunks)], ba_ref[...],
                          [wx_ref[c] for c in range(n_chunks)], bx_ref[...], lam_ref[...])
    a_cum, h_zero = _scan_rows(a, drive)
    h = a_cum * h_ref[0:1, :] + h_zero
    y_ref[...] = (jax.nn.gelu(gl_ref[...]) * h).astype(BF16)
    hist_ref[...] = xl
    last = h[rows - SUBLANES:, :]
    h_ref[...] = jnp.broadcast_to(last[SUBLANES - 1:, :], h_ref.shape)
    hlast_ref[...] = last
    ctail_ref[...] = xl[rows - SUBLANES:, :]


def _lru_prompt(grp, xg, lru_w):
    tt = T_LRU
    tiles = grp.seq // tt
    n_chunks = W_LRU // GATE_CHUNK

    def full(shape):
        return pl.BlockSpec(shape, lambda b, t: (0,) * len(shape))

    tail = pl.BlockSpec((None, SUBLANES, W_LRU), lambda b, t: (b, 0, 0))
    tail_shape = jax.ShapeDtypeStruct((grp.batch, SUBLANES, W_LRU), F32)
    y, hlast, ctail = pl.pallas_call(
        _lru_prompt_kernel,
        grid=(grp.batch, tiles),
        in_specs=[pl.BlockSpec((tt, W_LRU), lambda b, t: (b * tiles + t, 0)),
                  pl.BlockSpec((tt, W_LRU), lambda b, t: (b * tiles + t, 1)),
                  full((CONV_LRU, W_LRU)), full((1, W_LRU)), full((n_chunks, GATE_CHUNK, GATE_CHUNK)),
                  full((1, W_LRU)), full((n_chunks, GATE_CHUNK, GATE_CHUNK)), full((1, W_LRU)), full((1, W_LRU))],
        out_specs=[pl.BlockSpec((tt, W_LRU), lambda b, t: (b * tiles + t, 0)), tail, tail],
        out_shape=[jax.ShapeDtypeStruct((grp.rows, W_LRU), BF16), tail_shape, tail_shape],
        scratch_shapes=[pltpu.VMEM((tt, W_LRU), F32), pltpu.VMEM((SUBLANES, W_LRU), F32)],
        compiler_params=_params(("arbitrary", "arbitrary")),
        name="lru_prompt",
    )(xg, xg, *lru_w)
    return y, hlast[:, SUBLANES - 1, :], ctail[:, SUBLANES - (CONV_LRU - 1):, :]


def _lru_sample_kernel(xl_ref, gl_ref, hist_ref, h0_ref, cw_ref, cb_ref, wa_ref, ba_ref, wx_ref, bx_ref, lam_ref,
                       y_ref, h_ref, *, n_t):
    nb = h0_ref.shape[0]

    def slab(t):
        return slice(t * nb, (t + 1) * nb)

    def u(t):
        return xl_ref[slab(t), :] if t >= 0 else hist_ref[slab(CONV_LRU - 1 + t), :]

    xc = []
    for t in range(n_t):
        acc = u(t) * cw_ref[CONV_LRU - 1:CONV_LRU, :] + cb_ref[...]
        for s in range(1, CONV_LRU):
            acc = acc + u(t - s) * cw_ref[CONV_LRU - 1 - s:CONV_LRU - s, :]
        xc.append(acc)
    xc = jnp.concatenate(xc, axis=0)
    a, drive = _lru_gates(xc, [wa_ref[0]], ba_ref[...], [wx_ref[0]], bx_ref[...], lam_ref[...])
    h = h0_ref[...]
    for t in range(n_t):
        h = a[slab(t), :] * h + drive[slab(t), :]
        y_ref[slab(t), :] = (jax.nn.gelu(gl_ref[slab(t), :]) * h).astype(BF16)
    h_ref[...] = h


def _lru_sample(grp, xg, hist, h0, lru_w):
    gc = GATE_CHUNK
    n_chunks = W_LRU // gc
    vec = pl.BlockSpec((1, gc), lambda c: (0, c))
    gate_w = pl.BlockSpec((1, gc, gc), lambda c: (c, 0, 0))
    return pl.pallas_call(
        functools.partial(_lru_sample_kernel, n_t=grp.seq),
        grid=(n_chunks,),
        in_specs=[pl.BlockSpec((grp.rows, gc), lambda c: (0, c)),
                  pl.BlockSpec((grp.rows, gc), lambda c: (0, n_chunks + c)),
                  pl.BlockSpec((hist.shape[0], gc), lambda c: (0, c)),
                  pl.BlockSpec((grp.batch, gc), lambda c: (0, c)),
                  pl.BlockSpec((CONV_LRU, gc), lambda c: (0, c)), vec, gate_w, vec, gate_w, vec, vec],
        out_specs=[pl.BlockSpec((grp.rows, gc), lambda c: (0, c)), pl.BlockSpec((grp.batch, gc), lambda c: (0, c))],
        out_shape=[jax.ShapeDtypeStruct((grp.rows, W_LRU), BF16), jax.ShapeDtypeStruct((grp.batch, W_LRU), F32)],
        compiler_params=_params(("arbitrary",)),
        name="lru_sample",
    )(xg, xg, hist, h0, *lru_w)


def _residual_epilogue(acc, x_ref, gate_ref, norm, x_out_ref, h_out_ref):
    for sl in _mod_slabs(x_ref.shape[0], gate_ref.shape[0]):
        x_new = x_ref[sl, :] + gate_ref[...] * acc[sl, :]
        x_out_ref[sl, :] = x_new
        if h_out_ref is not None:
            g_ref, shift_ref, scale_ref = norm
            h_out_ref[sl, :] = _modnorm(x_new, g_ref[...], shift_ref[...], scale_ref[...]).astype(BF16)


def _out_proj_kernel(*refs, n_in):
    a_refs = refs[:n_in]
    w_ref, x_ref, gate_ref, g_ref, shift_ref, scale_ref, x_out_ref, h_out_ref = refs[n_in:]
    acc = None
    k0 = 0
    for a_ref in a_refs:
        k = a_ref.shape[1]
        part = _dot(a_ref[...].astype(BF16), w_ref[k0:k0 + k, :])
        acc = part if acc is None else acc + part
        k0 += k
    _residual_epilogue(acc, x_ref, gate_ref, (g_ref, shift_ref, scale_ref), x_out_ref, h_out_ref)


def _out_proj(grp, acts, w, x, gate_mod, g_next, next_mod, name):
    w, idx = w
    tm = TM_OUT
    row = pl.BlockSpec((tm, D_MODEL), lambda i: (i, 0))
    return pl.pallas_call(
        functools.partial(_out_proj_kernel, n_in=len(acts)),
        grid=(grp.rows // tm,),
        in_specs=[pl.BlockSpec((tm, a.shape[1]), lambda i: (i, 0)) for a in acts] + [
            pl.BlockSpec((None, D_MODEL, D_MODEL), lambda i: (idx, 0, 0)),
            row,
            grp.mod_spec(tm, 2),
            pl.BlockSpec((1, D_MODEL), lambda i: (0, 0)),
            grp.mod_spec(tm, 0),
            grp.mod_spec(tm, 1),
        ],
        out_specs=[row, row],
        out_shape=[jax.ShapeDtypeStruct((grp.rows, D_MODEL), F32), jax.ShapeDtypeStruct((grp.rows, D_MODEL), BF16)],
        compiler_params=_params(("arbitrary",)),
        name=name,
    )(*acts, w, x, gate_mod, g_next, next_mod, next_mod)


def _mlp_kernel(*refs, with_norm):
    if with_norm:
        h_ref, w1_ref, w2_ref, x_ref, gate_ref, g_ref, shift_ref, scale_ref, x_out_ref, h_out_ref = refs
        norm = (g_ref, shift_ref, scale_ref)
    else:
        h_ref, w1_ref, w2_ref, x_ref, gate_ref, x_out_ref = refs
        norm, h_out_ref = None, None
    f = pl.program_id(1)

    @pl.when(f == 0)
    def _():
        x_out_ref[...] = jnp.zeros_like(x_out_ref)

    hid = jnp.maximum(_dot(h_ref[...], w1_ref[...]), 0.0)
    x_out_ref[...] += _dot((hid * hid).astype(BF16), w2_ref[...])

    @pl.when(f == pl.num_programs(1) - 1)
    def _():
        _residual_epilogue(x_out_ref, x_ref, gate_ref, norm, x_out_ref, h_out_ref)


def _mlp(grp, h, w1, w2, layer, x, gate_mod, g_next, next_mod, name):
    tm, tf = TM_OUT, TF_MLP
    with_norm = g_next is not None
    row = pl.BlockSpec((tm, D_MODEL), lambda i, f: (i, 0))
    in_specs = [row,
                pl.BlockSpec((None, D_MODEL, tf), lambda i, f: (layer, 0, f)),
                pl.BlockSpec((None, tf, D_MODEL), lambda i, f: (layer, f, 0)),
                row,
                grp.mod_spec(tm, 2)]
    args = [h, w1, w2, x, gate_mod]
    out_specs = [row]
    out_shape = [jax.ShapeDtypeStruct((grp.rows, D_MODEL), F32)]
    if with_norm:
        in_specs += [pl.BlockSpec((1, D_MODEL), lambda i, f: (0, 0)), grp.mod_spec(tm, 0), grp.mod_spec(tm, 1)]
        args += [g_next, next_mod, next_mod]
        out_specs.append(row)
        out_shape.append(jax.ShapeDtypeStruct((grp.rows, D_MODEL), BF16))
    return pl.pallas_call(
        functools.partial(_mlp_kernel, with_norm=with_norm),
        grid=(grp.rows // tm, D_FF // tf),
        in_specs=in_specs,
        out_specs=out_specs,
        out_shape=out_shape,
        compiler_params=_params(("arbitrary", "arbitrary")),
        name=name,
    )(*args)


def _sconv_prompt_kernel(h_ref, wb_ref, wc_ref, wx_ref, cw_ref, g_ref, tail_ref, hist_ref, *, tiles_per_batch):
    i = pl.program_id(1)
    rows = h_ref.shape[0]

    @pl.when(i % tiles_per_batch == 0)
    def _():
        hist_ref[...] = jnp.zeros_like(hist_ref)

    h = h_ref[...]
    u = _dot(h, wc_ref[...]) * _dot(h, wx_ref[...])
    hist = hist_ref[...]
    conv = u * cw_ref[SC_CONV - 1:SC_CONV, :]
    for s in range(1, SC_CONV):
        conv = conv + _shifted(u, hist, s) * cw_ref[SC_CONV - 1 - s:SC_CONV - s, :]
    g_ref[...] = (_dot(h, wb_ref[...]) * conv).astype(BF16)
    hist_ref[...] = u
    tail_ref[...] = u[rows - SUBLANES:, :]


def _sconv_sample_kernel(h_ref, wb_ref, wc_ref, wx_ref, cw_ref, hist_ref, g_ref, tail_ref, *, n_t):
    h = h_ref[...]
    u = _dot(h, wc_ref[...]) * _dot(h, wx_ref[...])
    b = _dot(h, wb_ref[...])
    nb = h.shape[0] // n_t

    def slab(t):
        return slice(t * nb, (t + 1) * nb)

    def ut(t):
        return u[slab(t), :] if t >= 0 else hist_ref[slab(SC_CONV - 1 + t), :]

    for t in range(n_t):
        conv = ut(t) * cw_ref[SC_CONV - 1:SC_CONV, :]
        for s in range(1, SC_CONV):
            conv = conv + ut(t - s) * cw_ref[SC_CONV - 1 - s:SC_CONV - s, :]
        g_ref[slab(t), :] = (b[slab(t), :] * conv).astype(BF16)
    tail_ref[...] = u[(n_t - (SC_CONV - 1)) * nb:, :]


def _sconv_specs(tm, tn, idx):
    n_blocks = W_SC // tn
    return [pl.BlockSpec((tm, D_MODEL), lambda n, i: (i, 0)),
            pl.BlockSpec((None, D_MODEL, tn), lambda n, i: (idx, 0, n)),
            pl.BlockSpec((None, D_MODEL, tn), lambda n, i: (idx, 0, n_blocks + n)),
            pl.BlockSpec((None, D_MODEL, tn), lambda n, i: (idx, 0, 2 * n_blocks + n)),
            pl.BlockSpec((SC_CONV, tn), lambda n, i: (0, n))]


def _sconv_prompt(grp, h, w_in, conv_w):
    w_in, idx = w_in
    tm, tn = TM_PROJ, TN_ODD
    tiles_per_batch = grp.seq // tm
    g, tail = pl.pallas_call(
        functools.partial(_sconv_prompt_kernel, tiles_per_batch=tiles_per_batch),
        grid=(W_SC // tn, grp.rows // tm),
        in_specs=_sconv_specs(tm, tn, idx),
        out_specs=[pl.BlockSpec((tm, tn), lambda n, i: (i, n)),
                   pl.BlockSpec((None, SUBLANES, tn), lambda n, i: (i // tiles_per_batch, 0, n))],
        out_shape=[jax.ShapeDtypeStruct((grp.rows, W_SC), BF16),
                   jax.ShapeDtypeStruct((grp.batch, SUBLANES, W_SC), F32)],
        scratch_shapes=[pltpu.VMEM((tm, tn), F32)],
        compiler_params=_params(("arbitrary", "arbitrary")),
        name="sconv_prompt",
    )(h, w_in, w_in, w_in, conv_w)
    return g, tail[:, SUBLANES - (SC_CONV - 1):, :]


def _sconv_sample(grp, h, w_in, conv_w, hist):
    w_in, idx = w_in
    tm, tn = grp.rows, TN_ODD
    n_tail = (SC_CONV - 1) * grp.batch
    return pl.pallas_call(
        functools.partial(_sconv_sample_kernel, n_t=grp.seq),
        grid=(W_SC // tn, 1),
        in_specs=_sconv_specs(tm, tn, idx) + [pl.BlockSpec((n_tail, tn), lambda n, i: (0, n))],
        out_specs=[pl.BlockSpec((tm, tn), lambda n, i: (0, n)), pl.BlockSpec((n_tail, tn), lambda n, i: (0, n))],
        out_shape=[jax.ShapeDtypeStruct((grp.rows, W_SC), BF16), jax.ShapeDtypeStruct((n_tail, W_SC), F32)],
        compiler_params=_params(("arbitrary", "arbitrary")),
        name="sconv_sample",
    )(h, w_in, w_in, w_in, conv_w, hist)


def _rope_tables(pos):
    half = DH // 2
    inv = ROPE_THETA ** (-jnp.arange(half, dtype=F32) / half)
    ang = pos.astype(F32)[:, None] * inv[None, :]
    cos = jnp.cos(ang)
    sin = jnp.sin(ang)
    cos = jnp.concatenate([cos, cos, cos, cos], axis=-1)
    sin = jnp.concatenate([-sin, sin, -sin, sin], axis=-1)
    return cos, sin


def _block_diag(w):
    per = GATE_CHUNK // BLK_LRU
    w = w.reshape(W_LRU // GATE_CHUNK, per, BLK_LRU, BLK_LRU)
    eye = jnp.eye(per, dtype=w.dtype)
    out = jnp.einsum("cpij,pq->cpiqj", w, eye)
    return out.reshape(W_LRU // GATE_CHUNK, GATE_CHUNK, GATE_CHUNK).astype(BF16)


def _to_time_major(a):
    a = jnp.swapaxes(a, 0, 1)
    return a.reshape((a.shape[0] * a.shape[1],) + a.shape[2:])


def _from_time_major(a, batch):
    return jnp.swapaxes(a.reshape((a.shape[0] // batch, batch) + a.shape[1:]), 0, 1)


def _stack(parts):
    return parts[0][None] if len(parts) == 1 else jnp.stack(parts)


def _trunk(grp, x, mods, pos, state, kv_cache, W):
    lru_h0, lru_hist, sconv_hist = state
    outs = {n: [] for n in ("k", "v", "h", "cl", "sc")}
    unrow = (lambda a: _from_time_major(a, grp.batch)) if grp.time_major else (
        lambda a: a.reshape((grp.batch, a.shape[0] // grp.batch) + a.shape[1:]))
    h = _first_norm(grp, x, W["norm_mix"][0], mods["mix"][0])
    for layer in range(DEPTH):
        if layer % 2 == 0:
            e = layer // 2
            lam_init = 0.8 - 0.6 * math.exp(-0.3 * layer)
            cos, sin = _rope_tables(pos)
            if grp.time_major:
                cos, sin = jnp.repeat(cos, grp.batch, axis=0), jnp.repeat(sin, grp.batch, axis=0)
            q, kv, kf, vf, xg = _even_in_proj(grp, h, (W["w_in_even"], e), W["q_norm"][e], W["k_norm"][e], cos, sin)
            lam_vecs = [W[n][e][None, :] for n in ("lambda_q1", "lambda_k1", "lambda_q2", "lambda_k2")]
            subln = W["subln"][e][None, :]
            if kv_cache is None:
                o = _attn_prompt(grp, q[0], q[1], kv[0], kv[1], lam_vecs, subln, lam_init)
                yl, h_last, conv_tail = _lru_prompt(grp, xg, W["lru"][e])
            else:
                cache_k, cache_v, page_table = kv_cache
                o = _attn_sample(grp, e, q[0], kv[0], kv[1], cache_k, cache_v, page_table, lam_vecs, subln, lam_init)
                yl, h_last = _lru_sample(grp, xg, _to_time_major(lru_hist[e]), lru_h0[e], W["lru"][e])
                conv_tail = unrow(xg[(grp.seq - (CONV_LRU - 1)) * grp.batch:, :W_LRU])
            outs["k"].append(unrow(kf).reshape(grp.batch, grp.seq, H_ATT, 2 * DH))
            outs["v"].append(unrow(vf).reshape(grp.batch, grp.seq, H_ATT, DV))
            outs["h"].append(h_last)
            outs["cl"].append(conv_tail)
            acts, w_out = [o, yl], (W["w_out_even"], e)
        else:
            o_idx = layer // 2
            if kv_cache is None:
                g, sc = _sconv_prompt(grp, h, (W["w_in_odd"], o_idx), W["sconv_w"][o_idx])
            else:
                g, sc = _sconv_sample(grp, h, (W["w_in_odd"], o_idx), W["sconv_w"][o_idx],
                                      _to_time_major(sconv_hist[o_idx]))
                sc = unrow(sc)
            outs["sc"].append(sc)
            acts, w_out = [g], (W["w_out_odd"], o_idx)
        x, h = _out_proj(grp, acts, w_out, x, mods["mix"][layer], W["norm_mlp"][layer], mods["mlp"][layer],
                         "out_proj_%d" % layer)
        if layer + 1 < DEPTH:
            x, h = _mlp(grp, h, W["mlp_w1"], W["mlp_w2"], layer, x, mods["mlp"][layer],
                        W["norm_mix"][layer + 1], mods["mix"][layer + 1], "mlp_%d" % layer)
        else:
            (x,) = _mlp(grp, h, W["mlp_w1"], W["mlp_w2"], layer, x, mods["mlp"][layer], None, None,
                        "mlp_%d" % layer)
    return unrow(x), {k: _stack(v) for k, v in outs.items()}


def kernel(x_prompt, x_sample, cache_k, cache_v, state_lru_h, state_lru_conv, state_sconv, page_table, c_prompt, c_sample, norm_mix, norm_mlp, ada_mix_w, ada_mix_b, ada_mlp_w, ada_mlp_b, mlp_w1, mlp_w2, w_in_even, w_out_even, lru_conv_w, lru_conv_b, lru_wa, lru_ba, lru_wx, lru_bx, lru_lam, q_norm, k_norm, lambda_q1, lambda_k1, lambda_q2, lambda_k2, subln, w_in_odd, sconv_w, w_out_odd):
    bsz, seq, _ = x_prompt.shape
    dec_b, dec_seq, _ = x_sample.shape
    past_len = page_table.shape[1] * cache_k.shape[2]
    prompt = _Group(bsz, seq, time_major=False)
    sample = _Group(dec_b, dec_seq, time_major=True)

    n_c = bsz + dec_b
    pad = (-n_c) % (2 * SUBLANES)
    c_all = jnp.pad(jnp.concatenate([c_sample, c_prompt], axis=0), ((0, pad), (0, 0)))
    mods_p, mods_s = {}, {}
    for kind, w, b in (("mix", ada_mix_w, ada_mix_b), ("mlp", ada_mlp_w, ada_mlp_b)):
        m = _ada(c_all, w, b)
        mods_s[kind] = [m[l, :dec_b] for l in range(DEPTH)]
        mods_p[kind] = [m[l, dec_b:n_c].reshape(bsz, 1, 3 * D_MODEL) for l in range(DEPTH)]

    row = lambda a: a.reshape(a.shape[0], 1, a.shape[-1])
    W = {
        "norm_mix": row(norm_mix), "norm_mlp": row(norm_mlp),
        "mlp_w1": mlp_w1.astype(BF16), "mlp_w2": mlp_w2.astype(BF16),
        "w_in_even": w_in_even.astype(BF16), "w_out_even": w_out_even.astype(BF16),
        "w_in_odd": w_in_odd.astype(BF16), "w_out_odd": w_out_odd.astype(BF16),
        "sconv_w": sconv_w,
        "q_norm": jnp.tile(q_norm, (1, 2))[:, None, :], "k_norm": jnp.tile(k_norm, (1, 2))[:, None, :],
        "lambda_q1": lambda_q1, "lambda_k1": lambda_k1, "lambda_q2": lambda_q2, "lambda_k2": lambda_k2,
        "subln": subln,
        "lru": [(lru_conv_w[e], lru_conv_b[e][None, :], _block_diag(lru_wa[e]), lru_ba[e].reshape(1, W_LRU),
                 _block_diag(lru_wx[e]), lru_bx[e].reshape(1, W_LRU), lru_lam[e][None, :])
                for e in range(N_EVEN)],
    }

    y_p, o_p = _trunk(prompt, x_prompt.reshape(prompt.rows, D_MODEL), mods_p, jnp.arange(seq),
                      (None, None, None), None, W)
    y_s, o_s = _trunk(sample, _to_time_major(x_sample), mods_s, past_len + jnp.arange(dec_seq),
                      (state_lru_h, state_lru_conv, state_sconv), (cache_k, cache_v, page_table), W)
    return (y_p, y_s, o_p["k"], o_p["v"], o_s["k"], o_s["v"], o_p["h"], o_s["h"],
            o_p["cl"], o_s["cl"], o_p["sc"], o_s["sc"])
```

```python
import functools
import math

import jax
import jax.numpy as jnp
from jax import lax
from jax.experimental import pallas as pl
from jax.experimental.pallas import tpu as pltpu

D_MODEL = 2048
DEPTH = 2
PAGE_SIZE = 128
N_EVEN = (DEPTH + 1) // 2
N_ODD = DEPTH // 2
W_LRU = D_MODEL // 2
H_LRU = 16
BLK_LRU = W_LRU // H_LRU
CONV_LRU = 4
LRU_C = 8.0
H_ATT = 8
DV = (D_MODEL // 2) // H_ATT
DH = DV // 2
W_ATT = H_ATT * DV
W_SC = D_MODEL
SC_CONV = 3
D_FF = 4 * D_MODEL
ROPE_THETA = 10000.0
RMS_EPS = 1e-6
NEG_INF = -1e30

F32 = jnp.float32
BF16 = jnp.bfloat16

SUBLANES = 8
LANES = 128
MIB = 1 << 20
VMEM_LIMIT = 56 * MIB

TM_PROJ = 1024
TM_OUT = 512
TF_MLP = 1024
TN_ODD = 512
TN_ODD_CAST = 256
QK_COLS = 256
TQ_ATT = 256
HEADS_PER_STEP = 2
T_LRU = 256
GATE_CHUNK = 256
EPILOGUE_ROWS = 128


def _params(sem):
    return pltpu.CompilerParams(dimension_semantics=sem, vmem_limit_bytes=VMEM_LIMIT)


def _dot(a, b):
    return jnp.dot(a, b, preferred_element_type=F32)


def _dot_nt(a, b):
    return lax.dot_general(a, b, (((1,), (1,)), ((), ())), preferred_element_type=F32)


def _modnorm(x, g, shift, scale):
    ms = jnp.mean(x * x, axis=-1, keepdims=True)
    y = x * lax.rsqrt(ms + RMS_EPS)
    return (y * g) * (1.0 + scale) + shift


def _mod_slabs(n_rows, mod_rows):
    step = EPILOGUE_ROWS if mod_rows == 1 else mod_rows
    return [slice(r, r + step) for r in range(0, n_rows, step)]


def _ada_kernel(c_ref, w_ref, b_ref, o_ref):
    c = c_ref[...]
    a = (c * jax.nn.sigmoid(c)).astype(BF16)
    o_ref[...] = _dot(a, w_ref[...].astype(BF16)) + b_ref[...]


def _ada(c_all, w, b):
    rows = c_all.shape[0]
    tn = 1024
    return pl.pallas_call(
        _ada_kernel,
        grid=(DEPTH, 3 * D_MODEL // tn),
        in_specs=[
            pl.BlockSpec((rows, D_MODEL), lambda l, j: (0, 0)),
            pl.BlockSpec((None, D_MODEL, tn), lambda l, j: (l, 0, j)),
            pl.BlockSpec((None, 1, tn), lambda l, j: (l, 0, j)),
        ],
        out_specs=pl.BlockSpec((None, rows, tn), lambda l, j: (l, 0, j)),
        out_shape=jax.ShapeDtypeStruct((DEPTH, rows, 3 * D_MODEL), F32),
        compiler_params=_params(("arbitrary", "arbitrary")),
        name="ada_modulation",
    )(c_all, w, b.reshape(DEPTH, 1, 3 * D_MODEL))


class _Group:
    def __init__(self, batch, seq, time_major):
        self.batch = batch
        self.seq = seq
        self.rows = batch * seq
        self.time_major = time_major

    def mod_spec(self, tm, part, row_axis=0):
        if self.time_major:
            assert tm % self.batch == 0
            return pl.BlockSpec((self.batch, D_MODEL), lambda *g: (0, part))
        tiles_per_batch = self.seq // tm
        return pl.BlockSpec((None, 1, D_MODEL), lambda *g: (g[row_axis] // tiles_per_batch, 0, part))


def _modnorm_kernel(x_ref, g_ref, shift_ref, scale_ref, h_ref):
    for sl in _mod_slabs(x_ref.shape[0], shift_ref.shape[0]):
        h_ref[sl, :] = _modnorm(x_ref[sl, :], g_ref[...], shift_ref[...], scale_ref[...]).astype(BF16)


def _first_norm(grp, x, g, mod):
    tm = TM_OUT
    return pl.pallas_call(
        _modnorm_kernel,
        grid=(grp.rows // tm,),
        in_specs=[
            pl.BlockSpec((tm, D_MODEL), lambda i: (i, 0)),
            pl.BlockSpec((1, D_MODEL), lambda i: (0, 0)),
            grp.mod_spec(tm, 0),
            grp.mod_spec(tm, 1),
        ],
        out_specs=pl.BlockSpec((tm, D_MODEL), lambda i: (i, 0)),
        out_shape=jax.ShapeDtypeStruct((grp.rows, D_MODEL), BF16),
        compiler_params=_params(("arbitrary",)),
        name="first_norm",
    )(x, g, mod, mod)


def _weight_cols(w_ref, wb_ref, cols):
    w = w_ref[:, cols]
    if wb_ref is None:
        return w
    w = w.astype(BF16)
    wb_ref[:, cols] = w
    return w


def _qk_heads(h_ref, w_ref, wb_ref, g_ref, cos_ref, sin_ref):
    h = h_ref[...]
    g, cos, sin = g_ref[...], cos_ref[...], sin_ref[...]
    tm = h.shape[0]
    lane = lax.broadcasted_iota(jnp.int32, (tm, LANES), 1)
    first_map = lane < DH
    first_half = (lane % DH) < (DH // 2)
    out = []
    for c in range(W_ATT // QK_COLS):
        acc = _dot(h, _weight_cols(w_ref, wb_ref, slice(c * QK_COLS, (c + 1) * QK_COLS)))
        for hd in range(QK_COLS // LANES):
            xs = acc[:, hd * LANES:(hd + 1) * LANES]
            sq = xs * xs
            lo = jnp.sum(jnp.where(first_map, sq, 0.0), axis=-1, keepdims=True)
            hi = jnp.sum(jnp.where(first_map, 0.0, sq), axis=-1, keepdims=True)
            ms = jnp.where(first_map, lo, hi) * (1.0 / DH)
            y = xs * lax.rsqrt(ms + RMS_EPS) * g
            rot = jnp.where(first_half,
                            pltpu.roll(y, LANES - DH // 2, axis=1),
                            pltpu.roll(y, DH // 2, axis=1))
            out.append(y * cos + rot * sin)
    return out, first_map


def _q_prompt_kernel(h_ref, w_ref, g_ref, cos_ref, sin_ref, q1_ref, q2_ref):
    slabs, first_map = _qk_heads(h_ref, w_ref, None, g_ref, cos_ref, sin_ref)
    for hd, y in enumerate(slabs):
        y = y * (DH ** -0.5)
        cols = slice(hd * LANES, (hd + 1) * LANES)
        q1_ref[:, cols] = jnp.where(first_map, y, 0.0).astype(BF16)
        q2_ref[:, cols] = jnp.where(first_map, 0.0, y).astype(BF16)


def _k_prompt_kernel(h_ref, w_ref, g_ref, cos_ref, sin_ref, kf_ref, kb_ref):
    slabs, _ = _qk_heads(h_ref, w_ref, None, g_ref, cos_ref, sin_ref)
    for hd, y in enumerate(slabs):
        cols = slice(hd * LANES, (hd + 1) * LANES)
        kf_ref[:, cols] = y
        kb_ref[:, cols] = y.astype(BF16)


def _qk_sample_kernel(h_ref, w_ref, g_ref, cos_ref, sin_ref, o_ref, wb_ref, *, scale):
    slabs, _ = _qk_heads(h_ref, w_ref, wb_ref, g_ref, cos_ref, sin_ref)
    for hd, y in enumerate(slabs):
        o_ref[:, hd * LANES:(hd + 1) * LANES] = y * scale


def _v_prompt_kernel(h_ref, w_ref, vf_ref, vb_ref):
    acc = _dot(h_ref[...], w_ref[...])
    vf_ref[...] = acc
    vb_ref[...] = acc.astype(BF16)


def _plain_sample_kernel(h_ref, w_ref, o_ref, wb_ref):
    o_ref[...] = _dot(h_ref[...], _weight_cols(w_ref, wb_ref, slice(None)))


def _proj_call(kernel, grp, h, w, n_col_blocks, extra, extra_specs, out_dtypes, name):
    tm = min(TM_PROJ, grp.rows)
    tn = W_ATT
    out_spec = pl.BlockSpec((tm, tn), lambda i, j: (i, j))
    out_specs = [out_spec] * len(out_dtypes)
    out_shape = [jax.ShapeDtypeStruct((grp.rows, tn * n_col_blocks), dt) for dt in out_dtypes]
    w_cols = pl.BlockSpec((D_MODEL, tn), lambda i, j: (0, j))
    if isinstance(w, tuple):
        w, idx, col_block = w
        w_spec = pl.BlockSpec((None, D_MODEL, tn), lambda i, j: (idx, 0, col_block + j))
        out_specs.append(w_cols)
        out_shape.append(jax.ShapeDtypeStruct((D_MODEL, tn * n_col_blocks), BF16))
    else:
        w_spec = w_cols
    return pl.pallas_call(
        kernel,
        grid=(grp.rows // tm, n_col_blocks),
        in_specs=[pl.BlockSpec((tm, D_MODEL), lambda i, j: (i, 0)), w_spec] + extra_specs,
        out_specs=out_specs,
        out_shape=out_shape,
        compiler_params=_params(("arbitrary", "arbitrary")),
        name=name,
    )(h, w, *extra)


def _rope_specs(grp, cos):
    tm = min(TM_PROJ, grp.rows)
    tiles = cos.shape[0] // tm
    rope = pl.BlockSpec((tm, LANES), lambda i, j: (i % tiles, 0))
    return [pl.BlockSpec((1, LANES), lambda i, j: (0, 0)), rope, rope]


def _even_in_proj_sample(grp, h, w, e, qg, kg, cos, sin):
    specs = _rope_specs(grp, cos)
    q_kernel = functools.partial(_qk_sample_kernel, scale=DH ** -0.5)
    k_kernel = functools.partial(_qk_sample_kernel, scale=1.0)
    q, wq = _proj_call(q_kernel, grp, h, (w, e, 0), 1, (qg, cos, sin), specs, (F32,), "q_proj")
    kf, wk = _proj_call(k_kernel, grp, h, (w, e, 1), 1, (kg, cos, sin), specs, (F32,), "k_proj")
    vf, wv = _proj_call(_plain_sample_kernel, grp, h, (w, e, 2), 1, (), [], (F32,), "v_proj")
    xg, wl = _proj_call(_plain_sample_kernel, grp, h, (w, e, 3), 2, (), [], (F32,), "lru_proj")
    return q, kf, vf, xg, (wq, wk, wv, wl)


def _even_in_proj_prompt(grp, h, wq, wk, wv, qg, kg, cos, sin):
    specs = _rope_specs(grp, cos)
    q1, q2 = _proj_call(_q_prompt_kernel, grp, h, wq, 1, (qg, cos, sin), specs, (BF16, BF16), "q_proj")
    kf, kb = _proj_call(_k_prompt_kernel, grp, h, wk, 1, (kg, cos, sin), specs, (F32, BF16), "k_proj")
    vf, vb = _proj_call(_v_prompt_kernel, grp, h, wv, 1, (), [], (F32, BF16), "v_proj")
    return q1, q2, kf, kb, vf, vb


def _diff_lambda(lq1_ref, lk1_ref, lq2_ref, lk2_ref, lam_init):
    a = jnp.sum(lq1_ref[...] * lk1_ref[...], axis=-1, keepdims=True)
    b = jnp.sum(lq2_ref[...] * lk2_ref[...], axis=-1, keepdims=True)
    return jnp.exp(a) - jnp.exp(b) + lam_init


def _sub_norm(o, g, lam_init):
    ms = jnp.mean(o * o, axis=-1, keepdims=True)
    return (o * lax.rsqrt(ms + RMS_EPS) * g) * (1.0 - lam_init)


def _attn_prompt_kernel(lq1_ref, lk1_ref, lq2_ref, lk2_ref, g_ref, q1_ref, q2_ref, k_ref, v_ref, *cast_refs,
                        tq, n_tiles, lam_init):
    n_cast = len(cast_refs) // 2
    o_ref = cast_refs[n_cast]
    for src_ref, dst_ref in zip(cast_refs[:n_cast], cast_refs[n_cast + 1:]):
        dst_ref[...] = src_ref[...].astype(BF16)
    qi = pl.program_id(2)
    lam = _diff_lambda(lq1_ref, lk1_ref, lq2_ref, lk2_ref, lam_init)
    row = lax.broadcasted_iota(jnp.int32, (tq, tq), 0)
    col = lax.broadcasted_iota(jnp.int32, (tq, tq), 1)
    causal = col <= row

    def body(c):
        n_past = c * tq
        for hd in range(o_ref.shape[1] // DV):
            cols = slice(hd * DV, (hd + 1) * DV)
            kd = k_ref[n_past:n_past + tq, cols]
            vd = v_ref[n_past:n_past + tq, cols]
            maps = []
            for q_ref in (q1_ref, q2_ref):
                q = q_ref[:, cols]
                sd = jnp.where(causal, _dot_nt(q, kd), NEG_INF)
                m = jnp.max(sd, axis=-1, keepdims=True)
                if c > 0:
                    sp = _dot_nt(q, k_ref[0:n_past, cols])
                    m = jnp.maximum(m, jnp.max(sp, axis=-1, keepdims=True))
                pd = jnp.exp(sd - m)
                l = jnp.sum(pd, axis=-1, keepdims=True)
                acc = _dot(pd.astype(BF16), vd)
                if c > 0:
                    pp = jnp.exp(sp - m)
                    l = l + jnp.sum(pp, axis=-1, keepdims=True)
                    acc = acc + _dot(pp.astype(BF16), v_ref[0:n_past, cols])
                maps.append(acc / l)
            o = maps[0] - lam * maps[1]
            o_ref[:, cols] = _sub_norm(o, g_ref[...], lam_init).astype(BF16)

    for c in range(n_tiles):
        pl.when(qi == c)(functools.partial(body, c))


def _attn_prompt(grp, q1, q2, kb, vb, lam_vecs, subln, lam_init, to_cast):
    tq = TQ_ATT
    tiles = grp.seq // tq
    n_groups = H_ATT // HEADS_PER_STEP
    width = HEADS_PER_STEP * DV
    steps = grp.batch * n_groups * tiles
    vec64 = pl.BlockSpec((1, DH), lambda b, h, i: (0, 0))
    q_spec = pl.BlockSpec((tq, width), lambda b, h, i: (b * tiles + i, h))
    kv_spec = pl.BlockSpec((grp.seq, width), lambda b, h, i: (b, h))
    cast_specs = [pl.BlockSpec((a.shape[0] // steps, a.shape[1]), lambda b, h, i: ((b * n_groups + h) * tiles + i, 0))
                  for a in to_cast]
    assert all(a.shape[0] % (steps * 2 * SUBLANES) == 0 for a in to_cast)
    res = pl.pallas_call(
        functools.partial(_attn_prompt_kernel, tq=tq, n_tiles=tiles, lam_init=lam_init),
        grid=(grp.batch, n_groups, tiles),
        in_specs=[vec64] * 4 + [pl.BlockSpec((1, DV), lambda b, h, i: (0, 0)), q_spec, q_spec, kv_spec, kv_spec]
        + cast_specs,
        out_specs=[q_spec] + cast_specs,
        out_shape=[jax.ShapeDtypeStruct((grp.rows, W_ATT), BF16)]
        + [jax.ShapeDtypeStruct(a.shape, BF16) for a in to_cast],
        compiler_params=_params(("arbitrary", "arbitrary", "arbitrary")),
        name="attn_prompt",
    )(*lam_vecs, subln, q1, q2, kb, vb, *to_cast)
    return res[0], res[1:]


def _attn_sample_kernel(pt_ref, lq1_ref, lk1_ref, lq2_ref, lk2_ref, g_ref, q_ref, kn_ref, vn_ref, *rest,
                        n_pages, lam_init):
    k_refs = rest[:n_pages]
    v_refs = rest[n_pages:2 * n_pages]
    o_ref = rest[2 * n_pages]
    del pt_ref
    n_q = q_ref.shape[0]
    half = H_ATT // 2
    per_head = 2 * n_q
    n_past = n_pages * PAGE_SIZE
    lam = _diff_lambda(lq1_ref, lk1_ref, lq2_ref, lk2_ref, lam_init)
    g = g_ref[...]
    lane = lax.broadcasted_iota(jnp.int32, (n_q, LANES), 1)
    row = lax.broadcasted_iota(jnp.int32, (2 * per_head, 1), 0)
    t_of_row = row % n_q
    second = row >= per_head
    col = lax.broadcasted_iota(jnp.int32, (2 * per_head, 2 * n_past), 1)
    own_head = (col % 2) == (lax.broadcasted_iota(jnp.int32, (2 * per_head, 2 * n_past), 0) // per_head)
    for hp in range(half):
        heads = (hp, hp + half)
        qbd = []
        for hd in heads:
            q = q_ref[:, hd, :]
            qbd += [jnp.where(lane < DH, q, 0.0), jnp.where(lane < DH, 0.0, q)]
        qbd = jnp.concatenate(qbd, axis=0)
        pair_rows = pl.ds(hp, 2 * PAGE_SIZE, stride=half)
        kp = jnp.concatenate([r[pair_rows, :].astype(BF16) for r in k_refs], axis=0)
        vp = jnp.concatenate([r[pair_rows, :].astype(BF16) for r in v_refs], axis=0)
        s = jnp.where(own_head, _dot_nt(qbd.astype(BF16), kp), NEG_INF)
        kn = [kn_ref[:, hd, :] for hd in heads]
        vn = [vn_ref[:, hd, :] for hd in heads]
        s_new = []
        for j in range(n_q):
            kj = jnp.where(second, kn[1][j:j + 1, :], kn[0][j:j + 1, :])
            s_new.append(jnp.where(t_of_row >= j, jnp.sum(qbd * kj, axis=-1, keepdims=True), NEG_INF))
        m = jnp.max(s, axis=-1, keepdims=True)
        for sj in s_new:
            m = jnp.maximum(m, sj)
        p = jnp.exp(s - m)
        l = jnp.sum(p, axis=-1, keepdims=True)
        acc = _dot(p.astype(BF16), vp)
        for j, sj in enumerate(s_new):
            pj = jnp.exp(sj - m)
            l = l + pj
            acc = acc + pj * jnp.where(second, vn[1][j:j + 1, :], vn[0][j:j + 1, :])
        o = acc / l
        for a, hd in enumerate(heads):
            oa = o[a * per_head:a * per_head + n_q] - lam * o[a * per_head + n_q:(a + 1) * per_head]
            o_ref[:, hd, :] = _sub_norm(oa, g, lam_init)


def _attn_sample(grp, e, q, kf, vf, cache_k, cache_v, page_table, lam_vecs, subln, lam_init):
    n_pages = page_table.shape[1]
    tok = (grp.seq, grp.batch, H_ATT, LANES)
    vec64 = pl.BlockSpec((1, DH), lambda b, pt: (0, 0))
    new_spec = pl.BlockSpec((grp.seq, None, H_ATT, LANES), lambda b, pt: (0, b, 0, 0))

    def page_spec(p):
        return pl.BlockSpec((None, None, PAGE_SIZE * H_ATT, LANES),
                            lambda b, pt: (e, pt[b * n_pages + p], 0, 0))

    pages = [page_spec(p) for p in range(n_pages)]
    page_rows = cache_k.shape[:2] + (PAGE_SIZE * H_ATT, LANES)
    cache_k = cache_k.reshape(page_rows)
    cache_v = cache_v.reshape(page_rows)
    out = pl.pallas_call(
        functools.partial(_attn_sample_kernel, n_pages=n_pages, lam_init=lam_init),
        grid_spec=pltpu.PrefetchScalarGridSpec(
            num_scalar_prefetch=1,
            grid=(grp.batch,),
            in_specs=[vec64] * 4 + [pl.BlockSpec((1, DV), lambda b, pt: (0, 0)), new_spec, new_spec, new_spec]
            + pages + pages,
            out_specs=new_spec,
        ),
        out_shape=jax.ShapeDtypeStruct(tok, F32),
        compiler_params=_params(("arbitrary",)),
        name="attn_sample",
    )(page_table.reshape(-1), *lam_vecs, subln, q.reshape(tok), kf.reshape(tok), vf.reshape(tok),
      *([cache_k] * n_pages), *([cache_v] * n_pages))
    return out.reshape(grp.rows, W_ATT)


def _shifted(u, hist, s):
    row = lax.broadcasted_iota(jnp.int32, u.shape, 0)
    return jnp.where(row < s, pltpu.roll(hist, s, axis=0), pltpu.roll(u, s, axis=0))


def _scan_rows(a, b, carry):
    r, w = a.shape
    groups = r // SUBLANES
    a = a.reshape(groups, SUBLANES, w)
    b = b.reshape(groups, SUBLANES, w)
    sub = lax.broadcasted_iota(jnp.int32, a.shape, 1)
    s = 1
    while s < SUBLANES:
        valid = sub >= s
        b = jnp.where(valid, a * pltpu.roll(b, s, axis=1) + b, b)
        a = jnp.where(valid, a * pltpu.roll(a, s, axis=1), a)
        s *= 2
    hs = []
    for grp_i in range(groups):
        hg = a[grp_i] * carry + b[grp_i]
        hs.append(hg)
        carry = hg[SUBLANES - 1:, :]
    return hs


def _lru_gates(xc, wa, ba, wx, bx, lam):
    xb = xc.astype(BF16)
    r_parts, i_parts = [], []
    for c, (wa_c, wx_c) in enumerate(zip(wa, wx)):
        cols = slice(c * GATE_CHUNK, (c + 1) * GATE_CHUNK)
        r_parts.append(_dot(xb[:, cols], wa_c))
        i_parts.append(_dot(xb[:, cols], wx_c))
    cat = (lambda parts: parts[0] if len(parts) == 1 else jnp.concatenate(parts, axis=1))
    r = jax.nn.sigmoid(cat(r_parts) + ba)
    i = jax.nn.sigmoid(cat(i_parts) + bx)
    log_a = (-LRU_C) * r * jax.nn.softplus(-lam)
    a = jnp.exp(log_a)
    drive = jnp.sqrt(1.0 - a * a) * (i * xc)
    return a, drive


def _lru_prompt_kernel(h_ref, wl_ref, cw_ref, cb_ref, wa_ref, ba_ref, wx_ref, bx_ref, lam_ref,
                       y_ref, hlast_ref, ctail_ref, hist_ref, hprev_ref):
    t = pl.program_id(1)
    rows = h_ref.shape[0]

    @pl.when(t == 0)
    def _():
        hist_ref[...] = jnp.zeros_like(hist_ref)
        hprev_ref[...] = jnp.zeros_like(hprev_ref)

    hin = h_ref[...]
    xl = _dot(hin, wl_ref[:, :W_LRU])
    gl = _dot(hin, wl_ref[:, W_LRU:])
    hist = hist_ref[...]
    xc = xl * cw_ref[CONV_LRU - 1:CONV_LRU, :] + cb_ref[...]
    for s in range(1, CONV_LRU):
        xc = xc + _shifted(xl, hist, s) * cw_ref[CONV_LRU - 1 - s:CONV_LRU - s, :]
    n_chunks = W_LRU // GATE_CHUNK
    a, drive = _lru_gates(xc, [wa_ref[c] for c in range(n_chunks)], ba_ref[...],
                          [wx_ref[c] for c in range(n_chunks)], bx_ref[...], lam_ref[...])
    hs = _scan_rows(a, drive, hprev_ref[SUBLANES - 1:, :])
    y_ref[...] = (jax.nn.gelu(gl) * jnp.concatenate(hs, axis=0)).astype(BF16)
    hist_ref[...] = xl
    hprev_ref[...] = hs[-1]
    hlast_ref[...] = hs[-1]
    ctail_ref[...] = xl[rows - SUBLANES:, :]


def _lru_prompt(grp, h, wl, lru_w):
    tt = T_LRU
    tiles = grp.seq // tt
    n_chunks = W_LRU // GATE_CHUNK

    def full(shape):
        return pl.BlockSpec(shape, lambda b, t: (0,) * len(shape))

    tail = pl.BlockSpec((None, SUBLANES, W_LRU), lambda b, t: (b, 0, 0))
    tail_shape = jax.ShapeDtypeStruct((grp.batch, SUBLANES, W_LRU), F32)
    y, hlast, ctail = pl.pallas_call(
        _lru_prompt_kernel,
        grid=(grp.batch, tiles),
        in_specs=[pl.BlockSpec((tt, D_MODEL), lambda b, t: (b * tiles + t, 0)),
                  full((D_MODEL, 2 * W_LRU)),
                  full((CONV_LRU, W_LRU)), full((1, W_LRU)), full((n_chunks, GATE_CHUNK, GATE_CHUNK)),
                  full((1, W_LRU)), full((n_chunks, GATE_CHUNK, GATE_CHUNK)), full((1, W_LRU)), full((1, W_LRU))],
        out_specs=[pl.BlockSpec((tt, W_LRU), lambda b, t: (b * tiles + t, 0)), tail, tail],
        out_shape=[jax.ShapeDtypeStruct((grp.rows, W_LRU), BF16), tail_shape, tail_shape],
        scratch_shapes=[pltpu.VMEM((tt, W_LRU), F32), pltpu.VMEM((SUBLANES, W_LRU), F32)],
        compiler_params=_params(("arbitrary", "arbitrary")),
        name="lru_prompt",
    )(h, wl, *lru_w)
    return y, hlast[:, SUBLANES - 1, :], ctail[:, SUBLANES - (CONV_LRU - 1):, :]


def _lru_sample_kernel(xl_ref, gl_ref, hist_ref, h0_ref, cw_ref, cb_ref, wa_ref, ba_ref, wx_ref, bx_ref, lam_ref,
                       y_ref, h_ref, *, n_t):
    nb = h0_ref.shape[0]

    def slab(t):
        return slice(t * nb, (t + 1) * nb)

    def u(t):
        return xl_ref[slab(t), :] if t >= 0 else hist_ref[slab(CONV_LRU - 1 + t), :]

    xc = []
    for t in range(n_t):
        acc = u(t) * cw_ref[CONV_LRU - 1:CONV_LRU, :] + cb_ref[...]
        for s in range(1, CONV_LRU):
            acc = acc + u(t - s) * cw_ref[CONV_LRU - 1 - s:CONV_LRU - s, :]
        xc.append(acc)
    xc = jnp.concatenate(xc, axis=0)
    a, drive = _lru_gates(xc, [wa_ref[0]], ba_ref[...], [wx_ref[0]], bx_ref[...], lam_ref[...])
    h = h0_ref[...]
    for t in range(n_t):
        h = a[slab(t), :] * h + drive[slab(t), :]
        y_ref[slab(t), :] = (jax.nn.gelu(gl_ref[slab(t), :]) * h).astype(BF16)
    h_ref[...] = h


def _lru_sample(grp, xg, hist, h0, lru_w):
    gc = GATE_CHUNK
    n_chunks = W_LRU // gc
    vec = pl.BlockSpec((1, gc), lambda c: (0, c))
    gate_w = pl.BlockSpec((1, gc, gc), lambda c: (c, 0, 0))
    return pl.pallas_call(
        functools.partial(_lru_sample_kernel, n_t=grp.seq),
        grid=(n_chunks,),
        in_specs=[pl.BlockSpec((grp.rows, gc), lambda c: (0, c)),
                  pl.BlockSpec((grp.rows, gc), lambda c: (0, n_chunks + c)),
                  pl.BlockSpec((hist.shape[0], gc), lambda c: (0, c)),
                  pl.BlockSpec((grp.batch, gc), lambda c: (0, c)),
                  pl.BlockSpec((CONV_LRU, gc), lambda c: (0, c)), vec, gate_w, vec, gate_w, vec, vec],
        out_specs=[pl.BlockSpec((grp.rows, gc), lambda c: (0, c)), pl.BlockSpec((grp.batch, gc), lambda c: (0, c))],
        out_shape=[jax.ShapeDtypeStruct((grp.rows, W_LRU), BF16), jax.ShapeDtypeStruct((grp.batch, W_LRU), F32)],
        compiler_params=_params(("arbitrary",)),
        name="lru_sample",
    )(xg, xg, hist, h0, *lru_w)


def _residual_epilogue(acc, x_ref, gate_ref, norm, x_out_ref, h_out_ref):
    for sl in _mod_slabs(x_ref.shape[0], gate_ref.shape[0]):
        x_new = x_ref[sl, :] + gate_ref[...] * acc[sl, :]
        x_out_ref[sl, :] = x_new
        if h_out_ref is not None:
            g_ref, shift_ref, scale_ref = norm
            h_out_ref[sl, :] = _modnorm(x_new, g_ref[...], shift_ref[...], scale_ref[...]).astype(BF16)


def _out_proj_kernel(*refs, n_in):
    a_refs = refs[:n_in]
    w_ref, x_ref, gate_ref, g_ref, shift_ref, scale_ref, x_out_ref, h_out_ref = refs[n_in:]
    acc = None
    k0 = 0
    for a_ref in a_refs:
        k = a_ref.shape[1]
        part = _dot(a_ref[...].astype(BF16), w_ref[k0:k0 + k, :])
        acc = part if acc is None else acc + part
        k0 += k
    _residual_epilogue(acc, x_ref, gate_ref, (g_ref, shift_ref, scale_ref), x_out_ref, h_out_ref)


def _out_proj(grp, acts, w, x, gate_mod, g_next, next_mod, name):
    w, idx = w
    tm = TM_OUT
    row = pl.BlockSpec((tm, D_MODEL), lambda i: (i, 0))
    return pl.pallas_call(
        functools.partial(_out_proj_kernel, n_in=len(acts)),
        grid=(grp.rows // tm,),
        in_specs=[pl.BlockSpec((tm, a.shape[1]), lambda i: (i, 0)) for a in acts] + [
            pl.BlockSpec((None, D_MODEL, D_MODEL), lambda i: (idx, 0, 0)),
            row,
            grp.mod_spec(tm, 2),
            pl.BlockSpec((1, D_MODEL), lambda i: (0, 0)),
            grp.mod_spec(tm, 0),
            grp.mod_spec(tm, 1),
        ],
        out_specs=[row, row],
        out_shape=[jax.ShapeDtypeStruct((grp.rows, D_MODEL), F32), jax.ShapeDtypeStruct((grp.rows, D_MODEL), BF16)],
        compiler_params=_params(("arbitrary",)),
        name=name,
    )(*acts, w, x, gate_mod, g_next, next_mod, next_mod)


def _mlp_kernel(*refs, with_norm):
    if with_norm:
        h_ref, w1_ref, w2_ref, x_ref, gate_ref, g_ref, shift_ref, scale_ref, x_out_ref, h_out_ref = refs
        norm = (g_ref, shift_ref, scale_ref)
    else:
        h_ref, w1_ref, w2_ref, x_ref, gate_ref, x_out_ref = refs
        norm, h_out_ref = None, None
    f = pl.program_id(1)

    @pl.when(f == 0)
    def _():
        x_out_ref[...] = jnp.zeros_like(x_out_ref)

    hid = jnp.maximum(_dot(h_ref[...], w1_ref[...]), 0.0)
    x_out_ref[...] += _dot((hid * hid).astype(BF16), w2_ref[...])

    @pl.when(f == pl.num_programs(1) - 1)
    def _():
        _residual_epilogue(x_out_ref, x_ref, gate_ref, norm, x_out_ref, h_out_ref)


def _mlp(grp, h, w1, w2, layer, x, gate_mod, g_next, next_mod, name):
    tm, tf = TM_OUT, TF_MLP
    with_norm = g_next is not None
    row = pl.BlockSpec((tm, D_MODEL), lambda i, f: (i, 0))
    n_f = D_FF // tf
    in_specs = [row,
                pl.BlockSpec((D_MODEL, tf), lambda i, f: (layer, f)),
                pl.BlockSpec((tf, D_MODEL), lambda i, f: (layer * n_f + f, 0)),
                row,
                grp.mod_spec(tm, 2)]
    args = [h, w1, w2, x, gate_mod]
    out_specs = [row]
    out_shape = [jax.ShapeDtypeStruct((grp.rows, D_MODEL), F32)]
    if with_norm:
        in_specs += [pl.BlockSpec((1, D_MODEL), lambda i, f: (0, 0)), grp.mod_spec(tm, 0), grp.mod_spec(tm, 1)]
        args += [g_next, next_mod, next_mod]
        out_specs.append(row)
        out_shape.append(jax.ShapeDtypeStruct((grp.rows, D_MODEL), BF16))
    return pl.pallas_call(
        functools.partial(_mlp_kernel, with_norm=with_norm),
        grid=(grp.rows // tm, D_FF // tf),
        in_specs=in_specs,
        out_specs=out_specs,
        out_shape=out_shape,
        compiler_params=_params(("arbitrary", "arbitrary")),
        name=name,
    )(*args)


def _sconv_prompt_kernel(h_ref, wb_ref, wc_ref, wx_ref, cw_ref, g_ref, tail_ref, hist_ref, *, tiles_per_batch):
    i = pl.program_id(1)
    rows = h_ref.shape[0]

    @pl.when(i % tiles_per_batch == 0)
    def _():
        hist_ref[...] = jnp.zeros_like(hist_ref)

    h = h_ref[...]
    u = _dot(h, wc_ref[...]) * _dot(h, wx_ref[...])
    hist = hist_ref[...]
    conv = u * cw_ref[SC_CONV - 1:SC_CONV, :]
    for s in range(1, SC_CONV):
        conv = conv + _shifted(u, hist, s) * cw_ref[SC_CONV - 1 - s:SC_CONV - s, :]
    g_ref[...] = (_dot(h, wb_ref[...]) * conv).astype(BF16)
    hist_ref[...] = u
    tail_ref[...] = u[rows - SUBLANES:, :]


def _sconv_sample_kernel(h_ref, wb_ref, wc_ref, wx_ref, cw_ref, hist_ref, g_ref, tail_ref,
                         wb_out_ref, wc_out_ref, wx_out_ref, *, n_t):
    h = h_ref[...]
    every = slice(None)
    u = _dot(h, _weight_cols(wc_ref, wc_out_ref, every)) * _dot(h, _weight_cols(wx_ref, wx_out_ref, every))
    b = _dot(h, _weight_cols(wb_ref, wb_out_ref, every))
    nb = h.shape[0] // n_t

    def slab(t):
        return slice(t * nb, (t + 1) * nb)

    def ut(t):
        return u[slab(t), :] if t >= 0 else hist_ref[slab(SC_CONV - 1 + t), :]

    for t in range(n_t):
        conv = ut(t) * cw_ref[SC_CONV - 1:SC_CONV, :]
        for s in range(1, SC_CONV):
            conv = conv + ut(t - s) * cw_ref[SC_CONV - 1 - s:SC_CONV - s, :]
        g_ref[slab(t), :] = (b[slab(t), :] * conv).astype(BF16)
    tail_ref[...] = u[(n_t - (SC_CONV - 1)) * nb:, :]


def _sconv_prompt(grp, h, w_parts, conv_w):
    tm, tn = TM_PROJ, TN_ODD
    tiles_per_batch = grp.seq // tm
    w_cols = pl.BlockSpec((D_MODEL, tn), lambda n, i: (0, n))
    g, tail = pl.pallas_call(
        functools.partial(_sconv_prompt_kernel, tiles_per_batch=tiles_per_batch),
        grid=(W_SC // tn, grp.rows // tm),
        in_specs=[pl.BlockSpec((tm, D_MODEL), lambda n, i: (i, 0)), w_cols, w_cols, w_cols,
                  pl.BlockSpec((SC_CONV, tn), lambda n, i: (0, n))],
        out_specs=[pl.BlockSpec((tm, tn), lambda n, i: (i, n)),
                   pl.BlockSpec((None, SUBLANES, tn), lambda n, i: (i // tiles_per_batch, 0, n))],
        out_shape=[jax.ShapeDtypeStruct((grp.rows, W_SC), BF16),
                   jax.ShapeDtypeStruct((grp.batch, SUBLANES, W_SC), F32)],
        scratch_shapes=[pltpu.VMEM((tm, tn), F32)],
        compiler_params=_params(("arbitrary", "arbitrary")),
        name="sconv_prompt",
    )(h, *w_parts, conv_w)
    return g, tail[:, SUBLANES - (SC_CONV - 1):, :]


def _sconv_sample(grp, h, w_in, idx, conv_w, hist):
    tm, tn = grp.rows, TN_ODD_CAST
    n_blocks = W_SC // tn
    n_tail = (SC_CONV - 1) * grp.batch
    cols = pl.BlockSpec((D_MODEL, tn), lambda n: (0, n))
    w_shape = jax.ShapeDtypeStruct((D_MODEL, W_SC), BF16)
    g, tail, wb, wc, wx = pl.pallas_call(
        functools.partial(_sconv_sample_kernel, n_t=grp.seq),
        grid=(n_blocks,),
        in_specs=[pl.BlockSpec((tm, D_MODEL), lambda n: (0, 0)),
                  pl.BlockSpec((None, D_MODEL, tn), lambda n: (idx, 0, n)),
                  pl.BlockSpec((None, D_MODEL, tn), lambda n: (idx, 0, n_blocks + n)),
                  pl.BlockSpec((None, D_MODEL, tn), lambda n: (idx, 0, 2 * n_blocks + n)),
                  pl.BlockSpec((SC_CONV, tn), lambda n: (0, n)),
                  pl.BlockSpec((n_tail, tn), lambda n: (0, n))],
        out_specs=[pl.BlockSpec((tm, tn), lambda n: (0, n)), pl.BlockSpec((n_tail, tn), lambda n: (0, n)),
                   cols, cols, cols],
        out_shape=[jax.ShapeDtypeStruct((grp.rows, W_SC), BF16), jax.ShapeDtypeStruct((n_tail, W_SC), F32),
                   w_shape, w_shape, w_shape],
        compiler_params=_params(("arbitrary",)),
        name="sconv_sample",
    )(h, w_in, w_in, w_in, conv_w, hist)
    return g, tail, (wb, wc, wx)


def _rope_tables(pos):
    half = DH // 2
    inv = ROPE_THETA ** (-jnp.arange(half, dtype=F32) / half)
    ang = pos.astype(F32)[:, None] * inv[None, :]
    cos = jnp.cos(ang)
    sin = jnp.sin(ang)
    cos = jnp.concatenate([cos, cos, cos, cos], axis=-1)
    sin = jnp.concatenate([-sin, sin, -sin, sin], axis=-1)
    return cos, sin


def _block_diag(w):
    per = GATE_CHUNK // BLK_LRU
    w = w.reshape(W_LRU // GATE_CHUNK, per, BLK_LRU, BLK_LRU)
    eye = jnp.eye(per, dtype=w.dtype)
    out = jnp.einsum("cpij,pq->cpiqj", w, eye)
    return out.reshape(W_LRU // GATE_CHUNK, GATE_CHUNK, GATE_CHUNK).astype(BF16)


def _to_time_major(a):
    a = jnp.swapaxes(a, 0, 1)
    return a.reshape((a.shape[0] * a.shape[1],) + a.shape[2:])


def _from_time_major(a, batch):
    return jnp.swapaxes(a.reshape((a.shape[0] // batch, batch) + a.shape[1:]), 0, 1)


def _stack(parts):
    return parts[0][None] if len(parts) == 1 else jnp.stack(parts)


def _trunk(grp, x, mods, pos, state, kv_cache, W, cast):
    lru_h0, lru_hist, sconv_hist = state
    outs = {n: [] for n in ("k", "v", "h", "cl", "sc")}
    unrow = (lambda a: _from_time_major(a, grp.batch)) if grp.time_major else (
        lambda a: a.reshape((grp.batch, a.shape[0] // grp.batch) + a.shape[1:]))
    h = _first_norm(grp, x, W["norm_mix"][0], mods["mix"][0])
    for layer in range(DEPTH):
        if layer % 2 == 0:
            e = layer // 2
            lam_init = 0.8 - 0.6 * math.exp(-0.3 * layer)
            cos, sin = _rope_tables(pos)
            lam_vecs = [W[n][e][None, :] for n in ("lambda_q1", "lambda_k1", "lambda_q2", "lambda_k2")]
            subln = W["subln"][e][None, :]
            if grp.time_major:
                cos, sin = jnp.repeat(cos, grp.batch, axis=0), jnp.repeat(sin, grp.batch, axis=0)
                q, kf, vf, xg, cast["in_even", e] = _even_in_proj_sample(
                    grp, h, W["w_in_even"], e, W["q_norm"][e], W["k_norm"][e], cos, sin)
                cache_k, cache_v, page_table = kv_cache
                o = _attn_sample(grp, e, q, kf, vf, cache_k, cache_v, page_table, lam_vecs, subln, lam_init)
                yl, h_last = _lru_sample(grp, xg, _to_time_major(lru_hist[e]), lru_h0[e], W["lru"][e])
                conv_tail = unrow(xg[(grp.seq - (CONV_LRU - 1)) * grp.batch:, :W_LRU])
            else:
                yield from _await(cast, ("in_even", e))
                wq, wk, wv, wl = cast["in_even", e]
                q1, q2, kf, kb, vf, vb = _even_in_proj_prompt(grp, h, wq, wk, wv, W["q_norm"][e], W["k_norm"][e],
                                                              cos, sin)
                to_cast = (W["mlp_w1"], W["mlp_w2"]) if "mlp" not in cast else ()
                o, casted = _attn_prompt(grp, q1, q2, kb, vb, lam_vecs, subln, lam_init, to_cast)
                if to_cast:
                    cast["mlp"] = casted
                yl, h_last, conv_tail = _lru_prompt(grp, h, wl, W["lru"][e])
            outs["k"].append(unrow(kf).reshape(grp.batch, grp.seq, H_ATT, 2 * DH))
            outs["v"].append(unrow(vf).reshape(grp.batch, grp.seq, H_ATT, DV))
            outs["h"].append(h_last)
            outs["cl"].append(conv_tail)
            acts, w_out = [o, yl], (W["w_out_even"], e)
        else:
            o_idx = layer // 2
            if grp.time_major:
                g, sc, cast["in_odd", o_idx] = _sconv_sample(grp, h, W["w_in_odd"], o_idx, W["sconv_w"][o_idx],
                                                             _to_time_major(sconv_hist[o_idx]))
                sc = unrow(sc)
            else:
                yield from _await(cast, ("in_odd", o_idx))
                g, sc = _sconv_prompt(grp, h, cast["in_odd", o_idx], W["sconv_w"][o_idx])
            outs["sc"].append(sc)
            acts, w_out = [g], (W["w_out_odd"], o_idx)
        x, h = _out_proj(grp, acts, w_out, x, mods["mix"][layer], W["norm_mlp"][layer], mods["mlp"][layer],
                         "out_proj_%d" % layer)
        yield from _await(cast, "mlp")
        w1b, w2b = cast["mlp"]
        last = layer + 1 == DEPTH
        res = _mlp(grp, h, w1b, w2b, layer, x, mods["mlp"][layer], None if last else W["norm_mix"][layer + 1],
                   None if last else mods["mix"][layer + 1], "mlp_%d" % layer)
        x = res[0]
        h = None if last else res[1]
    return unrow(x), {k: _stack(v) for k, v in outs.items()}


def _await(cast, key):
    while key not in cast:
        yield key


def _run_trunks(trunks):
    results = {}
    while trunks:
        for name in list(trunks):
            try:
                next(trunks[name])
            except StopIteration as done:
                results[name] = done.value
                del trunks[name]
    return results


def kernel(x_prompt, x_sample, cache_k, cache_v, state_lru_h, state_lru_conv, state_sconv, page_table, c_prompt, c_sample, norm_mix, norm_mlp, ada_mix_w, ada_mix_b, ada_mlp_w, ada_mlp_b, mlp_w1, mlp_w2, w_in_even, w_out_even, lru_conv_w, lru_conv_b, lru_wa, lru_ba, lru_wx, lru_bx, lru_lam, q_norm, k_norm, lambda_q1, lambda_k1, lambda_q2, lambda_k2, subln, w_in_odd, sconv_w, w_out_odd):
    bsz, seq, _ = x_prompt.shape
    dec_b, dec_seq, _ = x_sample.shape
    past_len = page_table.shape[1] * cache_k.shape[2]
    prompt = _Group(bsz, seq, time_major=False)
    sample = _Group(dec_b, dec_seq, time_major=True)

    n_c = bsz + dec_b
    pad = (-n_c) % (2 * SUBLANES)
    c_all = jnp.pad(jnp.concatenate([c_sample, c_prompt], axis=0), ((0, pad), (0, 0)))
    mods_p, mods_s = {}, {}
    for kind, w, b in (("mix", ada_mix_w, ada_mix_b), ("mlp", ada_mlp_w, ada_mlp_b)):
        m = _ada(c_all, w, b)
        mods_s[kind] = [m[l, :dec_b] for l in range(DEPTH)]
        mods_p[kind] = [m[l, dec_b:n_c].reshape(bsz, 1, 3 * D_MODEL) for l in range(DEPTH)]

    row = lambda a: a.reshape(a.shape[0], 1, a.shape[-1])
    W = {
        "norm_mix": row(norm_mix), "norm_mlp": row(norm_mlp),
        "mlp_w1": mlp_w1.reshape(DEPTH * D_MODEL, D_FF), "mlp_w2": mlp_w2.reshape(DEPTH * D_FF, D_MODEL),
        "w_in_even": w_in_even, "w_in_odd": w_in_odd,
        "w_out_even": w_out_even.astype(BF16), "w_out_odd": w_out_odd.astype(BF16),
        "sconv_w": sconv_w,
        "q_norm": jnp.tile(q_norm, (1, 2))[:, None, :], "k_norm": jnp.tile(k_norm, (1, 2))[:, None, :],
        "lambda_q1": lambda_q1, "lambda_k1": lambda_k1, "lambda_q2": lambda_q2, "lambda_k2": lambda_k2,
        "subln": subln,
        "lru": [(lru_conv_w[e], lru_conv_b[e][None, :], _block_diag(lru_wa[e]), lru_ba[e].reshape(1, W_LRU),
                 _block_diag(lru_wx[e]), lru_bx[e].reshape(1, W_LRU), lru_lam[e][None, :])
                for e in range(N_EVEN)],
    }

    cast = {}
    done = _run_trunks({
        "sample": _trunk(sample, _to_time_major(x_sample), mods_s, past_len + jnp.arange(dec_seq),
                         (state_lru_h, state_lru_conv, state_sconv), (cache_k, cache_v, page_table), W, cast),
        "prompt": _trunk(prompt, x_prompt.reshape(prompt.rows, D_MODEL), mods_p, jnp.arange(seq),
                         (None, None, None), None, W, cast),
    })
    (y_s, o_s), (y_p, o_p) = done["sample"], done["prompt"]
    return (y_p, y_s, o_p["k"], o_p["v"], o_s["k"], o_s["v"], o_p["h"], o_s["h"],
            o_p["cl"], o_s["cl"], o_p["sc"], o_s["sc"])
```

```python
import functools
import math

import jax
import jax.numpy as jnp
from jax import lax
from jax.experimental import pallas as pl
from jax.experimental.pallas import tpu as pltpu

D_MODEL = 2048
DEPTH = 2
PAGE_SIZE = 128
N_EVEN = (DEPTH + 1) // 2
N_ODD = DEPTH // 2
W_LRU = D_MODEL // 2
H_LRU = 16
BLK_LRU = W_LRU // H_LRU
CONV_LRU = 4
LRU_C = 8.0
H_ATT = 8
DV = (D_MODEL // 2) // H_ATT
DH = DV // 2
W_ATT = H_ATT * DV
W_SC = D_MODEL
SC_CONV = 3
D_FF = 4 * D_MODEL
ROPE_THETA = 10000.0
RMS_EPS = 1e-6
NEG_INF = -1e30

F32 = jnp.float32
BF16 = jnp.bfloat16

SUBLANES = 8
LANES = 128
MIB = 1 << 20
VMEM_LIMIT = 56 * MIB

TM_PROJ = 1024
TM_OUT = 512
TF_MLP = 1024
TN_ODD = 512
TN_ODD_CAST = 256
QK_COLS = 256
TQ_ATT = 256
HEADS_PER_STEP = 4
T_LRU = 512
GATE_CHUNK = 256
EPILOGUE_ROWS = 16


def _params(sem):
    return pltpu.CompilerParams(dimension_semantics=sem, vmem_limit_bytes=VMEM_LIMIT)


def _dot(a, b):
    return jnp.dot(a, b, preferred_element_type=F32)


def _dot_nt(a, b):
    return lax.dot_general(a, b, (((1,), (1,)), ((), ())), preferred_element_type=F32)


def _modnorm(x, g, shift, scale):
    ms = jnp.mean(x * x, axis=-1, keepdims=True)
    y = x * lax.rsqrt(ms + RMS_EPS)
    return (y * g) * (1.0 + scale) + shift


def _mod_slabs(n_rows, mod_rows):
    out = []
    for r in range(0, n_rows, EPILOGUE_ROWS):
        m = slice(None) if mod_rows == 1 else slice(r % mod_rows, r % mod_rows + EPILOGUE_ROWS)
        out.append((slice(r, r + EPILOGUE_ROWS), m))
    return out


def _modnorm_rows(x_of, g, shift_ref, scale_ref, mrows):
    parts = []
    for lo in range(0, EPILOGUE_ROWS, SUBLANES):
        if isinstance(mrows, slice) and mrows.start is not None:
            m = slice(mrows.start + lo, mrows.start + lo + SUBLANES)
        else:
            m = mrows
        parts.append(_modnorm(x_of(lo, lo + SUBLANES), g, shift_ref[m, :], scale_ref[m, :]))
    return jnp.concatenate(parts, axis=0).astype(BF16)


def _ada_kernel(c_ref, w_ref, b_ref, o_ref):
    c = c_ref[...]
    a = (c * jax.nn.sigmoid(c)).astype(BF16)
    o_ref[...] = _dot(a, w_ref[...].astype(BF16)) + b_ref[...]


def _ada(c_all, w, b):
    rows = c_all.shape[0]
    tn = 1024
    return pl.pallas_call(
        _ada_kernel,
        grid=(DEPTH, 3 * D_MODEL // tn),
        in_specs=[
            pl.BlockSpec((rows, D_MODEL), lambda l, j: (0, 0)),
            pl.BlockSpec((None, D_MODEL, tn), lambda l, j: (l, 0, j)),
            pl.BlockSpec((None, 1, tn), lambda l, j: (l, 0, j)),
        ],
        out_specs=pl.BlockSpec((None, rows, tn), lambda l, j: (l, 0, j)),
        out_shape=jax.ShapeDtypeStruct((DEPTH, rows, 3 * D_MODEL), F32),
        compiler_params=_params(("arbitrary", "arbitrary")),
        name="ada_modulation",
    )(c_all, w, b.reshape(DEPTH, 1, 3 * D_MODEL))


class _Group:
    def __init__(self, batch, seq, time_major):
        self.batch = batch
        self.seq = seq
        self.rows = batch * seq
        self.time_major = time_major

    def mod_spec(self, tm, part, row_axis=0):
        if self.time_major:
            assert tm % self.batch == 0
            return pl.BlockSpec((self.batch, D_MODEL), lambda *g: (0, part))
        tiles_per_batch = self.seq // tm
        return pl.BlockSpec((None, 1, D_MODEL), lambda *g: (g[row_axis] // tiles_per_batch, 0, part))


def _modnorm_kernel(x_ref, g_ref, shift_ref, scale_ref, h_ref):
    g = g_ref[...]
    for rows, mrows in _mod_slabs(x_ref.shape[0], shift_ref.shape[0]):
        x_of = lambda lo, hi, r0=rows.start: x_ref[r0 + lo:r0 + hi, :]
        h_ref[rows, :] = _modnorm_rows(x_of, g, shift_ref, scale_ref, mrows)


def _first_norm(grp, x, g, mod):
    tm = TM_OUT
    return pl.pallas_call(
        _modnorm_kernel,
        grid=(grp.rows // tm,),
        in_specs=[
            pl.BlockSpec((tm, D_MODEL), lambda i: (i, 0)),
            pl.BlockSpec((1, D_MODEL), lambda i: (0, 0)),
            grp.mod_spec(tm, 0),
            grp.mod_spec(tm, 1),
        ],
        out_specs=pl.BlockSpec((tm, D_MODEL), lambda i: (i, 0)),
        out_shape=jax.ShapeDtypeStruct((grp.rows, D_MODEL), BF16),
        compiler_params=_params(("arbitrary",)),
        name="first_norm",
    )(x, g, mod, mod)


def _weight_cols(w_ref, wb_ref, cols):
    w = w_ref[:, cols]
    if wb_ref is None:
        return w
    w = w.astype(BF16)
    wb_ref[:, cols] = w
    return w


def _qk_heads(h_ref, w_ref, wb_ref, g_ref, cos_ref, sin_ref):
    h = h_ref[...]
    g, cos, sin = g_ref[...], cos_ref[...], sin_ref[...]
    tm = h.shape[0]
    lane = lax.broadcasted_iota(jnp.int32, (tm, LANES), 1)
    first_map = lane < DH
    first_half = (lane % DH) < (DH // 2)
    out = []
    for c in range(W_ATT // QK_COLS):
        acc = _dot(h, _weight_cols(w_ref, wb_ref, slice(c * QK_COLS, (c + 1) * QK_COLS)))
        for hd in range(QK_COLS // LANES):
            xs = acc[:, hd * LANES:(hd + 1) * LANES]
            sq = xs * xs
            lo = jnp.sum(jnp.where(first_map, sq, 0.0), axis=-1, keepdims=True)
            hi = jnp.sum(jnp.where(first_map, 0.0, sq), axis=-1, keepdims=True)
            ms = jnp.where(first_map, lo, hi) * (1.0 / DH)
            y = xs * lax.rsqrt(ms + RMS_EPS) * g
            rot = jnp.where(first_half,
                            pltpu.roll(y, LANES - DH // 2, axis=1),
                            pltpu.roll(y, DH // 2, axis=1))
            out.append(y * cos + rot * sin)
    return out, first_map


def _q_prompt_kernel(h_ref, w_ref, g_ref, cos_ref, sin_ref, q1_ref, q2_ref):
    slabs, first_map = _qk_heads(h_ref, w_ref, None, g_ref, cos_ref, sin_ref)
    for hd, y in enumerate(slabs):
        y = y * (DH ** -0.5)
        cols = slice(hd * LANES, (hd + 1) * LANES)
        q1_ref[:, cols] = jnp.where(first_map, y, 0.0).astype(BF16)
        q2_ref[:, cols] = jnp.where(first_map, 0.0, y).astype(BF16)


def _k_prompt_kernel(h_ref, w_ref, g_ref, cos_ref, sin_ref, kf_ref, kb_ref):
    slabs, _ = _qk_heads(h_ref, w_ref, None, g_ref, cos_ref, sin_ref)
    for hd, y in enumerate(slabs):
        cols = slice(hd * LANES, (hd + 1) * LANES)
        kf_ref[:, cols] = y
        kb_ref[:, cols] = y.astype(BF16)


def _qk_sample_kernel(h_ref, w_ref, g_ref, cos_ref, sin_ref, o_ref, wb_ref, *, scale):
    slabs, _ = _qk_heads(h_ref, w_ref, wb_ref, g_ref, cos_ref, sin_ref)
    for hd, y in enumerate(slabs):
        o_ref[:, hd * LANES:(hd + 1) * LANES] = y * scale


def _v_prompt_kernel(h_ref, w_ref, vf_ref, vb_ref):
    acc = _dot(h_ref[...], w_ref[...])
    vf_ref[...] = acc
    vb_ref[...] = acc.astype(BF16)


def _plain_sample_kernel(h_ref, w_ref, o_ref, wb_ref):
    o_ref[...] = _dot(h_ref[...], _weight_cols(w_ref, wb_ref, slice(None)))


def _proj_call(kernel, grp, h, w, n_col_blocks, extra, extra_specs, out_dtypes, name):
    tm = min(TM_PROJ, grp.rows)
    tn = W_ATT
    out_spec = pl.BlockSpec((tm, tn), lambda i, j: (i, j))
    out_specs = [out_spec] * len(out_dtypes)
    out_shape = [jax.ShapeDtypeStruct((grp.rows, tn * n_col_blocks), dt) for dt in out_dtypes]
    w_cols = pl.BlockSpec((D_MODEL, tn), lambda i, j: (0, j))
    if isinstance(w, tuple):
        w, idx, col_block = w
        w_spec = pl.BlockSpec((None, D_MODEL, tn), lambda i, j: (idx, 0, col_block + j))
        out_specs.append(w_cols)
        out_shape.append(jax.ShapeDtypeStruct((D_MODEL, tn * n_col_blocks), BF16))
    else:
        w_spec = w_cols
    return pl.pallas_call(
        kernel,
        grid=(grp.rows // tm, n_col_blocks),
        in_specs=[pl.BlockSpec((tm, D_MODEL), lambda i, j: (i, 0)), w_spec] + extra_specs,
        out_specs=out_specs,
        out_shape=out_shape,
        compiler_params=_params(("arbitrary", "arbitrary")),
        name=name,
    )(h, w, *extra)


def _rope_specs(grp, cos):
    tm = min(TM_PROJ, grp.rows)
    tiles = cos.shape[0] // tm
    rope = pl.BlockSpec((tm, LANES), lambda i, j: (i % tiles, 0))
    return [pl.BlockSpec((1, LANES), lambda i, j: (0, 0)), rope, rope]


def _even_in_proj_sample(grp, h, w, e, qg, kg, cos, sin):
    specs = _rope_specs(grp, cos)
    q_kernel = functools.partial(_qk_sample_kernel, scale=DH ** -0.5)
    k_kernel = functools.partial(_qk_sample_kernel, scale=1.0)
    q, wq = _proj_call(q_kernel, grp, h, (w, e, 0), 1, (qg, cos, sin), specs, (F32,), "q_proj")
    kf, wk = _proj_call(k_kernel, grp, h, (w, e, 1), 1, (kg, cos, sin), specs, (F32,), "k_proj")
    vf, wv = _proj_call(_plain_sample_kernel, grp, h, (w, e, 2), 1, (), [], (F32,), "v_proj")
    xg, wl = _proj_call(_plain_sample_kernel, grp, h, (w, e, 3), 2, (), [], (F32,), "lru_proj")
    return q, kf, vf, xg, (wq, wk, wv, wl)


def _even_in_proj_prompt(grp, h, wq, wk, wv, qg, kg, cos, sin):
    specs = _rope_specs(grp, cos)
    q1, q2 = _proj_call(_q_prompt_kernel, grp, h, wq, 1, (qg, cos, sin), specs, (BF16, BF16), "q_proj")
    kf, kb = _proj_call(_k_prompt_kernel, grp, h, wk, 1, (kg, cos, sin), specs, (F32, BF16), "k_proj")
    vf, vb = _proj_call(_v_prompt_kernel, grp, h, wv, 1, (), [], (F32, BF16), "v_proj")
    return q1, q2, kf, kb, vf, vb


def _diff_lambda(lq1_ref, lk1_ref, lq2_ref, lk2_ref, lam_init):
    a = jnp.sum(lq1_ref[...] * lk1_ref[...], axis=-1, keepdims=True)
    b = jnp.sum(lq2_ref[...] * lk2_ref[...], axis=-1, keepdims=True)
    return jnp.exp(a) - jnp.exp(b) + lam_init


def _sub_norm(o, g, lam_init):
    ms = jnp.mean(o * o, axis=-1, keepdims=True)
    return (o * lax.rsqrt(ms + RMS_EPS) * g) * (1.0 - lam_init)


def _attn_prompt_kernel(lq1_ref, lk1_ref, lq2_ref, lk2_ref, g_ref, q1_ref, q2_ref, k_ref, v_ref, *cast_refs,
                        tq, n_tiles, lam_init):
    n_cast = len(cast_refs) // 2
    o_ref = cast_refs[n_cast]
    for src_ref, dst_ref in zip(cast_refs[:n_cast], cast_refs[n_cast + 1:]):
        dst_ref[...] = src_ref[...].astype(BF16)
    qi = pl.program_id(2)
    lam = _diff_lambda(lq1_ref, lk1_ref, lq2_ref, lk2_ref, lam_init)
    row = lax.broadcasted_iota(jnp.int32, (tq, tq), 0)
    col = lax.broadcasted_iota(jnp.int32, (tq, tq), 1)
    causal = col <= row

    def body(c):
        n_past = c * tq
        for hd in range(o_ref.shape[1] // DV):
            cols = slice(hd * DV, (hd + 1) * DV)
            kd = k_ref[n_past:n_past + tq, cols]
            vd = v_ref[n_past:n_past + tq, cols]
            maps = []
            for q_ref in (q1_ref, q2_ref):
                q = q_ref[:, cols]
                sd = jnp.where(causal, _dot_nt(q, kd), NEG_INF)
                m = jnp.max(sd, axis=-1, keepdims=True)
                if c > 0:
                    sp = _dot_nt(q, k_ref[0:n_past, cols])
                    m = jnp.maximum(m, jnp.max(sp, axis=-1, keepdims=True))
                pd = jnp.exp(sd - m)
                l = jnp.sum(pd, axis=-1, keepdims=True)
                acc = _dot(pd.astype(BF16), vd)
                if c > 0:
                    pp = jnp.exp(sp - m)
                    l = l + jnp.sum(pp, axis=-1, keepdims=True)
                    acc = acc + _dot(pp.astype(BF16), v_ref[0:n_past, cols])
                maps.append(acc / l)
            o = maps[0] - lam * maps[1]
            o_ref[:, cols] = _sub_norm(o, g_ref[...], lam_init).astype(BF16)

    for c in range(n_tiles):
        pl.when(qi == c)(functools.partial(body, c))


def _attn_prompt(grp, q1, q2, kb, vb, lam_vecs, subln, lam_init, to_cast):
    tq = TQ_ATT
    tiles = grp.seq // tq
    n_groups = H_ATT // HEADS_PER_STEP
    width = HEADS_PER_STEP * DV
    steps = grp.batch * n_groups * tiles
    vec64 = pl.BlockSpec((1, DH), lambda b, h, i: (0, 0))
    q_spec = pl.BlockSpec((tq, width), lambda b, h, i: (b * tiles + i, h))
    kv_spec = pl.BlockSpec((grp.seq, width), lambda b, h, i: (b, h))
    cast_specs = [pl.BlockSpec((a.shape[0] // steps, a.shape[1]), lambda b, h, i: ((b * n_groups + h) * tiles + i, 0))
                  for a in to_cast]
    assert all(a.shape[0] % (steps * 2 * SUBLANES) == 0 for a in to_cast)
    res = pl.pallas_call(
        functools.partial(_attn_prompt_kernel, tq=tq, n_tiles=tiles, lam_init=lam_init),
        grid=(grp.batch, n_groups, tiles),
        in_specs=[vec64] * 4 + [pl.BlockSpec((1, DV), lambda b, h, i: (0, 0)), q_spec, q_spec, kv_spec, kv_spec]
        + cast_specs,
        out_specs=[q_spec] + cast_specs,
        out_shape=[jax.ShapeDtypeStruct((grp.rows, W_ATT), BF16)]
        + [jax.ShapeDtypeStruct(a.shape, BF16) for a in to_cast],
        compiler_params=_params(("arbitrary", "arbitrary", "arbitrary")),
        name="attn_prompt",
    )(*lam_vecs, subln, q1, q2, kb, vb, *to_cast)
    return res[0], res[1:]


def _attn_sample_kernel(pt_ref, lq1_ref, lk1_ref, lq2_ref, lk2_ref, g_ref, q_ref, kn_ref, vn_ref, *rest,
                        n_pages, lam_init):
    k_refs = rest[:n_pages]
    v_refs = rest[n_pages:2 * n_pages]
    o_ref = rest[2 * n_pages]
    del pt_ref
    n_q = q_ref.shape[0]
    half = H_ATT // 2
    per_head = 2 * n_q
    n_past = n_pages * PAGE_SIZE
    lam = _diff_lambda(lq1_ref, lk1_ref, lq2_ref, lk2_ref, lam_init)
    g = g_ref[...]
    lane = lax.broadcasted_iota(jnp.int32, (n_q, LANES), 1)
    row = lax.broadcasted_iota(jnp.int32, (2 * per_head, 1), 0)
    t_of_row = row % n_q
    second = row >= per_head
    col = lax.broadcasted_iota(jnp.int32, (2 * per_head, 2 * n_past), 1)
    own_head = (col % 2) == (lax.broadcasted_iota(jnp.int32, (2 * per_head, 2 * n_past), 0) // per_head)
    for hp in range(half):
        heads = (hp, hp + half)
        qbd = []
        for hd in heads:
            q = q_ref[:, hd, :]
            qbd += [jnp.where(lane < DH, q, 0.0), jnp.where(lane < DH, 0.0, q)]
        qbd = jnp.concatenate(qbd, axis=0)
        pair_rows = pl.ds(hp, 2 * PAGE_SIZE, stride=half)
        kp = jnp.concatenate([r[pair_rows, :].astype(BF16) for r in k_refs], axis=0)
        vp = jnp.concatenate([r[pair_rows, :].astype(BF16) for r in v_refs], axis=0)
        s = jnp.where(own_head, _dot_nt(qbd.astype(BF16), kp), NEG_INF)
        kn = [kn_ref[:, hd, :] for hd in heads]
        vn = [vn_ref[:, hd, :] for hd in heads]
        s_new = []
        for j in range(n_q):
            kj = jnp.where(second, kn[1][j:j + 1, :], kn[0][j:j + 1, :])
            s_new.append(jnp.where(t_of_row >= j, jnp.sum(qbd * kj, axis=-1, keepdims=True), NEG_INF))
        m = jnp.max(s, axis=-1, keepdims=True)
        for sj in s_new:
            m = jnp.maximum(m, sj)
        p = jnp.exp(s - m)
        l = jnp.sum(p, axis=-1, keepdims=True)
        acc = _dot(p.astype(BF16), vp)
        for j, sj in enumerate(s_new):
            pj = jnp.exp(sj - m)
            l = l + pj
            acc = acc + pj * jnp.where(second, vn[1][j:j + 1, :], vn[0][j:j + 1, :])
        o = acc / l
        for a, hd in enumerate(heads):
            oa = o[a * per_head:a * per_head + n_q] - lam * o[a * per_head + n_q:(a + 1) * per_head]
            o_ref[:, hd, :] = _sub_norm(oa, g, lam_init)


def _attn_sample(grp, e, q, kf, vf, cache_k, cache_v, page_table, lam_vecs, subln, lam_init):
    n_pages = page_table.shape[1]
    tok = (grp.seq, grp.batch, H_ATT, LANES)
    vec64 = pl.BlockSpec((1, DH), lambda b, pt: (0, 0))
    new_spec = pl.BlockSpec((grp.seq, None, H_ATT, LANES), lambda b, pt: (0, b, 0, 0))

    def page_spec(p):
        return pl.BlockSpec((None, None, PAGE_SIZE * H_ATT, LANES),
                            lambda b, pt: (e, pt[b * n_pages + p], 0, 0))

    pages = [page_spec(p) for p in range(n_pages)]
    page_rows = cache_k.shape[:2] + (PAGE_SIZE * H_ATT, LANES)
    cache_k = cache_k.reshape(page_rows)
    cache_v = cache_v.reshape(page_rows)
    out = pl.pallas_call(
        functools.partial(_attn_sample_kernel, n_pages=n_pages, lam_init=lam_init),
        grid_spec=pltpu.PrefetchScalarGridSpec(
            num_scalar_prefetch=1,
            grid=(grp.batch,),
            in_specs=[vec64] * 4 + [pl.BlockSpec((1, DV), lambda b, pt: (0, 0)), new_spec, new_spec, new_spec]
            + pages + pages,
            out_specs=new_spec,
        ),
        out_shape=jax.ShapeDtypeStruct(tok, F32),
        compiler_params=_params(("arbitrary",)),
        name="attn_sample",
    )(page_table.reshape(-1), *lam_vecs, subln, q.reshape(tok), kf.reshape(tok), vf.reshape(tok),
      *([cache_k] * n_pages), *([cache_v] * n_pages))
    return out.reshape(grp.rows, W_ATT)


def _shifted(u, hist, s):
    row = lax.broadcasted_iota(jnp.int32, u.shape, 0)
    return jnp.where(row < s, pltpu.roll(hist, s, axis=0), pltpu.roll(u, s, axis=0))


def _scan_rows(a, b, carry):
    r, w = a.shape
    groups = r // SUBLANES
    a = a.reshape(groups, SUBLANES, w)
    b = b.reshape(groups, SUBLANES, w)
    sub = lax.broadcasted_iota(jnp.int32, a.shape, 1)
    s = 1
    while s < SUBLANES:
        valid = sub >= s
        b = jnp.where(valid, a * pltpu.roll(b, s, axis=1) + b, b)
        a = jnp.where(valid, a * pltpu.roll(a, s, axis=1), a)
        s *= 2
    hs = []
    for grp_i in range(groups):
        hg = a[grp_i] * carry + b[grp_i]
        hs.append(hg)
        carry = hg[SUBLANES - 1:, :]
    return hs


def _lru_gates(xc, wa, ba, wx, bx, lam):
    xb = xc.astype(BF16)
    r_parts, i_parts = [], []
    for c, (wa_c, wx_c) in enumerate(zip(wa, wx)):
        cols = slice(c * GATE_CHUNK, (c + 1) * GATE_CHUNK)
        r_parts.append(_dot(xb[:, cols], wa_c))
        i_parts.append(_dot(xb[:, cols], wx_c))
    cat = (lambda parts: parts[0] if len(parts) == 1 else jnp.concatenate(parts, axis=1))
    r = jax.nn.sigmoid(cat(r_parts) + ba)
    i = jax.nn.sigmoid(cat(i_parts) + bx)
    log_a = (-LRU_C) * r * jax.nn.softplus(-lam)
    a = jnp.exp(log_a)
    drive = jnp.sqrt(1.0 - a * a) * (i * xc)
    return a, drive


def _lru_prompt_kernel(h_ref, wl_ref, cw_ref, cb_ref, wa_ref, ba_ref, wx_ref, bx_ref, lam_ref,
                       y_ref, hlast_ref, ctail_ref, hist_ref, hprev_ref):
    t = pl.program_id(1)
    rows = h_ref.shape[0]

    @pl.when(t == 0)
    def _():
        hist_ref[...] = jnp.zeros_like(hist_ref)
        hprev_ref[...] = jnp.zeros_like(hprev_ref)

    hin = h_ref[...]
    xl = _dot(hin, wl_ref[:, :W_LRU])
    gl = _dot(hin, wl_ref[:, W_LRU:])
    hist = hist_ref[...]
    xc = xl * cw_ref[CONV_LRU - 1:CONV_LRU, :] + cb_ref[...]
    for s in range(1, CONV_LRU):
        xc = xc + _shifted(xl, hist, s) * cw_ref[CONV_LRU - 1 - s:CONV_LRU - s, :]
    n_chunks = W_LRU // GATE_CHUNK
    a, drive = _lru_gates(xc, [wa_ref[c] for c in range(n_chunks)], ba_ref[...],
                          [wx_ref[c] for c in range(n_chunks)], bx_ref[...], lam_ref[...])
    hs = _scan_rows(a, drive, hprev_ref[SUBLANES - 1:, :])
    y_ref[...] = (jax.nn.gelu(gl) * jnp.concatenate(hs, axis=0)).astype(BF16)
    hist_ref[...] = xl
    hprev_ref[...] = hs[-1]
    hlast_ref[...] = hs[-1]
    ctail_ref[...] = xl[rows - SUBLANES:, :]


def _lru_prompt(grp, h, wl, lru_w):
    tt = T_LRU
    tiles = grp.seq // tt
    n_chunks = W_LRU // GATE_CHUNK

    def full(shape):
        return pl.BlockSpec(shape, lambda b, t: (0,) * len(shape))

    tail = pl.BlockSpec((None, SUBLANES, W_LRU), lambda b, t: (b, 0, 0))
    tail_shape = jax.ShapeDtypeStruct((grp.batch, SUBLANES, W_LRU), F32)
    y, hlast, ctail = pl.pallas_call(
        _lru_prompt_kernel,
        grid=(grp.batch, tiles),
        in_specs=[pl.BlockSpec((tt, D_MODEL), lambda b, t: (b * tiles + t, 0)),
                  full((D_MODEL, 2 * W_LRU)),
                  full((CONV_LRU, W_LRU)), full((1, W_LRU)), full((n_chunks, GATE_CHUNK, GATE_CHUNK)),
                  full((1, W_LRU)), full((n_chunks, GATE_CHUNK, GATE_CHUNK)), full((1, W_LRU)), full((1, W_LRU))],
        out_specs=[pl.BlockSpec((tt, W_LRU), lambda b, t: (b * tiles + t, 0)), tail, tail],
        out_shape=[jax.ShapeDtypeStruct((grp.rows, W_LRU), BF16), tail_shape, tail_shape],
        scratch_shapes=[pltpu.VMEM((tt, W_LRU), F32), pltpu.VMEM((SUBLANES, W_LRU), F32)],
        compiler_params=_params(("arbitrary", "arbitrary")),
        name="lru_prompt",
    )(h, wl, *lru_w)
    return y, hlast[:, SUBLANES - 1, :], ctail[:, SUBLANES - (CONV_LRU - 1):, :]


def _lru_sample_kernel(xl_ref, gl_ref, hist_ref, h0_ref, cw_ref, cb_ref, wa_ref, ba_ref, wx_ref, bx_ref, lam_ref,
                       y_ref, h_ref, *, n_t):
    nb = h0_ref.shape[0]

    def slab(t):
        return slice(t * nb, (t + 1) * nb)

    def u(t):
        return xl_ref[slab(t), :] if t >= 0 else hist_ref[slab(CONV_LRU - 1 + t), :]

    xc = []
    for t in range(n_t):
        acc = u(t) * cw_ref[CONV_LRU - 1:CONV_LRU, :] + cb_ref[...]
        for s in range(1, CONV_LRU):
            acc = acc + u(t - s) * cw_ref[CONV_LRU - 1 - s:CONV_LRU - s, :]
        xc.append(acc)
    xc = jnp.concatenate(xc, axis=0)
    a, drive = _lru_gates(xc, [wa_ref[0]], ba_ref[...], [wx_ref[0]], bx_ref[...], lam_ref[...])
    h = h0_ref[...]
    for t in range(n_t):
        h = a[slab(t), :] * h + drive[slab(t), :]
        y_ref[slab(t), :] = (jax.nn.gelu(gl_ref[slab(t), :]) * h).astype(BF16)
    h_ref[...] = h


def _lru_sample(grp, xg, hist, h0, lru_w):
    gc = GATE_CHUNK
    n_chunks = W_LRU // gc
    vec = pl.BlockSpec((1, gc), lambda c: (0, c))
    gate_w = pl.BlockSpec((1, gc, gc), lambda c: (c, 0, 0))
    return pl.pallas_call(
        functools.partial(_lru_sample_kernel, n_t=grp.seq),
        grid=(n_chunks,),
        in_specs=[pl.BlockSpec((grp.rows, gc), lambda c: (0, c)),
                  pl.BlockSpec((grp.rows, gc), lambda c: (0, n_chunks + c)),
                  pl.BlockSpec((hist.shape[0], gc), lambda c: (0, c)),
                  pl.BlockSpec((grp.batch, gc), lambda c: (0, c)),
                  pl.BlockSpec((CONV_LRU, gc), lambda c: (0, c)), vec, gate_w, vec, gate_w, vec, vec],
        out_specs=[pl.BlockSpec((grp.rows, gc), lambda c: (0, c)), pl.BlockSpec((grp.batch, gc), lambda c: (0, c))],
        out_shape=[jax.ShapeDtypeStruct((grp.rows, W_LRU), BF16), jax.ShapeDtypeStruct((grp.batch, W_LRU), F32)],
        compiler_params=_params(("arbitrary",)),
        name="lru_sample",
    )(xg, xg, hist, h0, *lru_w)


def _residual_epilogue(acc, x_ref, gate_ref, norm, x_out_ref, h_out_ref):
    g = None if norm is None else norm[0][...]
    for rows, mrows in _mod_slabs(x_ref.shape[0], gate_ref.shape[0]):
        r0 = rows.start

        def x_new_of(lo, hi, r0=r0, mrows=mrows):
            m = mrows if mrows.start is None else slice(mrows.start + lo, mrows.start + hi)
            x_new = x_ref[r0 + lo:r0 + hi, :] + gate_ref[m, :] * acc[r0 + lo:r0 + hi, :]
            x_out_ref[r0 + lo:r0 + hi, :] = x_new
            return x_new

        if h_out_ref is not None:
            h_out_ref[rows, :] = _modnorm_rows(x_new_of, g, norm[1], norm[2], mrows)
        else:
            for lo in range(0, EPILOGUE_ROWS, SUBLANES):
                x_new_of(lo, lo + SUBLANES)


def _out_proj_kernel(*refs, n_in):
    a_refs = refs[:n_in]
    w_ref, x_ref, gate_ref, g_ref, shift_ref, scale_ref, x_out_ref, h_out_ref = refs[n_in:]
    acc = None
    k0 = 0
    for a_ref in a_refs:
        k = a_ref.shape[1]
        part = _dot(a_ref[...].astype(BF16), w_ref[k0:k0 + k, :])
        acc = part if acc is None else acc + part
        k0 += k
    _residual_epilogue(acc, x_ref, gate_ref, (g_ref, shift_ref, scale_ref), x_out_ref, h_out_ref)


def _out_proj(grp, acts, w, x, gate_mod, g_next, next_mod, name):
    w, idx = w
    tm = TM_OUT
    row = pl.BlockSpec((tm, D_MODEL), lambda i: (i, 0))
    return pl.pallas_call(
        functools.partial(_out_proj_kernel, n_in=len(acts)),
        grid=(grp.rows // tm,),
        in_specs=[pl.BlockSpec((tm, a.shape[1]), lambda i: (i, 0)) for a in acts] + [
            pl.BlockSpec((D_MODEL, D_MODEL), lambda i: (idx, 0)),
            row,
            grp.mod_spec(tm, 2),
            pl.BlockSpec((1, D_MODEL), lambda i: (0, 0)),
            grp.mod_spec(tm, 0),
            grp.mod_spec(tm, 1),
        ],
        out_specs=[row, row],
        out_shape=[jax.ShapeDtypeStruct((grp.rows, D_MODEL), F32), jax.ShapeDtypeStruct((grp.rows, D_MODEL), BF16)],
        compiler_params=_params(("arbitrary",)),
        name=name,
    )(*acts, w, x, gate_mod, g_next, next_mod, next_mod)


def _mlp_kernel(*refs, with_norm):
    if with_norm:
        h_ref, w1_ref, w2_ref, x_ref, gate_ref, g_ref, shift_ref, scale_ref, x_out_ref, h_out_ref = refs
        norm = (g_ref, shift_ref, scale_ref)
    else:
        h_ref, w1_ref, w2_ref, x_ref, gate_ref, x_out_ref = refs
        norm, h_out_ref = None, None
    f = pl.program_id(1)

    @pl.when(f == 0)
    def _():
        x_out_ref[...] = jnp.zeros_like(x_out_ref)

    hid = jnp.maximum(_dot(h_ref[...], w1_ref[...]), 0.0)
    x_out_ref[...] += _dot((hid * hid).astype(BF16), w2_ref[...])

    @pl.when(f == pl.num_programs(1) - 1)
    def _():
        _residual_epilogue(x_out_ref, x_ref, gate_ref, norm, x_out_ref, h_out_ref)


def _mlp(grp, h, w1, w2, layer, x, gate_mod, g_next, next_mod, name):
    tm, tf = TM_OUT, TF_MLP
    with_norm = g_next is not None
    row = pl.BlockSpec((tm, D_MODEL), lambda i, f: (i, 0))
    n_f = D_FF // tf
    in_specs = [row,
                pl.BlockSpec((D_MODEL, tf), lambda i, f: (layer, f)),
                pl.BlockSpec((tf, D_MODEL), lambda i, f: (layer * n_f + f, 0)),
                row,
                grp.mod_spec(tm, 2)]
    args = [h, w1, w2, x, gate_mod]
    out_specs = [row]
    out_shape = [jax.ShapeDtypeStruct((grp.rows, D_MODEL), F32)]
    if with_norm:
        in_specs += [pl.BlockSpec((1, D_MODEL), lambda i, f: (0, 0)), grp.mod_spec(tm, 0), grp.mod_spec(tm, 1)]
        args += [g_next, next_mod, next_mod]
        out_specs.append(row)
        out_shape.append(jax.ShapeDtypeStruct((grp.rows, D_MODEL), BF16))
    return pl.pallas_call(
        functools.partial(_mlp_kernel, with_norm=with_norm),
        grid=(grp.rows // tm, D_FF // tf),
        in_specs=in_specs,
        out_specs=out_specs,
        out_shape=out_shape,
        compiler_params=_params(("arbitrary", "arbitrary")),
        name=name,
    )(*args)


def _sconv_prompt_kernel(h_ref, wb_ref, wc_ref, wx_ref, cw_ref, g_ref, tail_ref, hist_ref, *, tiles_per_batch):
    i = pl.program_id(1)
    rows = h_ref.shape[0]

    @pl.when(i % tiles_per_batch == 0)
    def _():
        hist_ref[...] = jnp.zeros_like(hist_ref)

    h = h_ref[...]
    u = _dot(h, wc_ref[...]) * _dot(h, wx_ref[...])
    hist = hist_ref[...]
    conv = u * cw_ref[SC_CONV - 1:SC_CONV, :]
    for s in range(1, SC_CONV):
        conv = conv + _shifted(u, hist, s) * cw_ref[SC_CONV - 1 - s:SC_CONV - s, :]
    g_ref[...] = (_dot(h, wb_ref[...]) * conv).astype(BF16)
    hist_ref[...] = u
    tail_ref[...] = u[rows - SUBLANES:, :]


def _sconv_sample_kernel(h_ref, wb_ref, wc_ref, wx_ref, cw_ref, hist_ref, g_ref, tail_ref,
                         wb_out_ref, wc_out_ref, wx_out_ref, *, n_t):
    h = h_ref[...]
    every = slice(None)
    u = _dot(h, _weight_cols(wc_ref, wc_out_ref, every)) * _dot(h, _weight_cols(wx_ref, wx_out_ref, every))
    b = _dot(h, _weight_cols(wb_ref, wb_out_ref, every))
    nb = h.shape[0] // n_t

    def slab(t):
        return slice(t * nb, (t + 1) * nb)

    def ut(t):
        return u[slab(t), :] if t >= 0 else hist_ref[slab(SC_CONV - 1 + t), :]

    for t in range(n_t):
        conv = ut(t) * cw_ref[SC_CONV - 1:SC_CONV, :]
        for s in range(1, SC_CONV):
            conv = conv + ut(t - s) * cw_ref[SC_CONV - 1 - s:SC_CONV - s, :]
        g_ref[slab(t), :] = (b[slab(t), :] * conv).astype(BF16)
    tail_ref[...] = u[(n_t - (SC_CONV - 1)) * nb:, :]


def _sconv_prompt(grp, h, w_parts, conv_w):
    tm, tn = TM_PROJ, TN_ODD
    tiles_per_batch = grp.seq // tm
    w_cols = pl.BlockSpec((D_MODEL, tn), lambda n, i: (0, n))
    g, tail = pl.pallas_call(
        functools.partial(_sconv_prompt_kernel, tiles_per_batch=tiles_per_batch),
        grid=(W_SC // tn, grp.rows // tm),
        in_specs=[pl.BlockSpec((tm, D_MODEL), lambda n, i: (i, 0)), w_cols, w_cols, w_cols,
                  pl.BlockSpec((SC_CONV, tn), lambda n, i: (0, n))],
        out_specs=[pl.BlockSpec((tm, tn), lambda n, i: (i, n)),
                   pl.BlockSpec((None, SUBLANES, tn), lambda n, i: (i // tiles_per_batch, 0, n))],
        out_shape=[jax.ShapeDtypeStruct((grp.rows, W_SC), BF16),
                   jax.ShapeDtypeStruct((grp.batch, SUBLANES, W_SC), F32)],
        scratch_shapes=[pltpu.VMEM((tm, tn), F32)],
        compiler_params=_params(("arbitrary", "arbitrary")),
        name="sconv_prompt",
    )(h, *w_parts, conv_w)
    return g, tail[:, SUBLANES - (SC_CONV - 1):, :]


def _sconv_sample(grp, h, w_in, idx, conv_w, hist):
    tm, tn = grp.rows, TN_ODD_CAST
    n_blocks = W_SC // tn
    n_tail = (SC_CONV - 1) * grp.batch
    cols = pl.BlockSpec((D_MODEL, tn), lambda n: (0, n))
    w_shape = jax.ShapeDtypeStruct((D_MODEL, W_SC), BF16)
    g, tail, wb, wc, wx = pl.pallas_call(
        functools.partial(_sconv_sample_kernel, n_t=grp.seq),
        grid=(n_blocks,),
        in_specs=[pl.BlockSpec((tm, D_MODEL), lambda n: (0, 0)),
                  pl.BlockSpec((None, D_MODEL, tn), lambda n: (idx, 0, n)),
                  pl.BlockSpec((None, D_MODEL, tn), lambda n: (idx, 0, n_blocks + n)),
                  pl.BlockSpec((None, D_MODEL, tn), lambda n: (idx, 0, 2 * n_blocks + n)),
                  pl.BlockSpec((SC_CONV, tn), lambda n: (0, n)),
                  pl.BlockSpec((n_tail, tn), lambda n: (0, n))],
        out_specs=[pl.BlockSpec((tm, tn), lambda n: (0, n)), pl.BlockSpec((n_tail, tn), lambda n: (0, n)),
                   cols, cols, cols],
        out_shape=[jax.ShapeDtypeStruct((grp.rows, W_SC), BF16), jax.ShapeDtypeStruct((n_tail, W_SC), F32),
                   w_shape, w_shape, w_shape],
        compiler_params=_params(("arbitrary",)),
        name="sconv_sample",
    )(h, w_in, w_in, w_in, conv_w, hist)
    return g, tail, (wb, wc, wx)


def _rope_tables(pos):
    half = DH // 2
    inv = ROPE_THETA ** (-jnp.arange(half, dtype=F32) / half)
    ang = pos.astype(F32)[:, None] * inv[None, :]
    cos = jnp.cos(ang)
    sin = jnp.sin(ang)
    cos = jnp.concatenate([cos, cos, cos, cos], axis=-1)
    sin = jnp.concatenate([-sin, sin, -sin, sin], axis=-1)
    return cos, sin


def _block_diag(w):
    per = GATE_CHUNK // BLK_LRU
    w = w.reshape(W_LRU // GATE_CHUNK, per, BLK_LRU, BLK_LRU)
    eye = jnp.eye(per, dtype=w.dtype)
    out = jnp.einsum("cpij,pq->cpiqj", w, eye)
    return out.reshape(W_LRU // GATE_CHUNK, GATE_CHUNK, GATE_CHUNK).astype(BF16)


def _to_time_major(a):
    a = jnp.swapaxes(a, 0, 1)
    return a.reshape((a.shape[0] * a.shape[1],) + a.shape[2:])


def _from_time_major(a, batch):
    return jnp.swapaxes(a.reshape((a.shape[0] // batch, batch) + a.shape[1:]), 0, 1)


def _stack(parts):
    return parts[0][None] if len(parts) == 1 else jnp.stack(parts)


def _trunk(grp, x, mods, pos, state, kv_cache, W, cast):
    lru_h0, lru_hist, sconv_hist = state
    outs = {n: [] for n in ("k", "v", "h", "cl", "sc")}
    unrow = (lambda a: _from_time_major(a, grp.batch)) if grp.time_major else (
        lambda a: a.reshape((grp.batch, a.shape[0] // grp.batch) + a.shape[1:]))
    h = _first_norm(grp, x, W["norm_mix"][0], mods["mix"][0])
    for layer in range(DEPTH):
        if layer % 2 == 0:
            e = layer // 2
            lam_init = 0.8 - 0.6 * math.exp(-0.3 * layer)
            cos, sin = _rope_tables(pos)
            lam_vecs = [W[n][e][None, :] for n in ("lambda_q1", "lambda_k1", "lambda_q2", "lambda_k2")]
            subln = W["subln"][e][None, :]
            if grp.time_major:
                cos, sin = jnp.repeat(cos, grp.batch, axis=0), jnp.repeat(sin, grp.batch, axis=0)
                q, kf, vf, xg, cast["in_even", e] = _even_in_proj_sample(
                    grp, h, W["w_in_even"], e, W["q_norm"][e], W["k_norm"][e], cos, sin)
                cache_k, cache_v, page_table = kv_cache
                o = _attn_sample(grp, e, q, kf, vf, cache_k, cache_v, page_table, lam_vecs, subln, lam_init)
                yl, h_last = _lru_sample(grp, xg, _to_time_major(lru_hist[e]), lru_h0[e], W["lru"][e])
                conv_tail = unrow(xg[(grp.seq - (CONV_LRU - 1)) * grp.batch:, :W_LRU])
            else:
                yield from _await(cast, ("in_even", e))
                wq, wk, wv, wl = cast["in_even", e]
                q1, q2, kf, kb, vf, vb = _even_in_proj_prompt(grp, h, wq, wk, wv, W["q_norm"][e], W["k_norm"][e],
                                                              cos, sin)
                to_cast = () if "mlp" in cast else (W["mlp_w1"], W["mlp_w2"], W["w_out_even"], W["w_out_odd"])
                o, casted = _attn_prompt(grp, q1, q2, kb, vb, lam_vecs, subln, lam_init, to_cast)
                if to_cast:
                    cast["mlp"] = casted[:2]
                    cast["w_out_even"], cast["w_out_odd"] = casted[2:]
                yl, h_last, conv_tail = _lru_prompt(grp, h, wl, W["lru"][e])
            outs["k"].append(unrow(kf).reshape(grp.batch, grp.seq, H_ATT, 2 * DH))
            outs["v"].append(unrow(vf).reshape(grp.batch, grp.seq, H_ATT, DV))
            outs["h"].append(h_last)
            outs["cl"].append(conv_tail)
            acts, w_out = [o, yl], ("w_out_even", e)
        else:
            o_idx = layer // 2
            if grp.time_major:
                g, sc, cast["in_odd", o_idx] = _sconv_sample(grp, h, W["w_in_odd"], o_idx, W["sconv_w"][o_idx],
                                                             _to_time_major(sconv_hist[o_idx]))
                sc = unrow(sc)
            else:
                yield from _await(cast, ("in_odd", o_idx))
                g, sc = _sconv_prompt(grp, h, cast["in_odd", o_idx], W["sconv_w"][o_idx])
            outs["sc"].append(sc)
            acts, w_out = [g], ("w_out_odd", o_idx)
        yield from _await(cast, w_out[0])
        x, h = _out_proj(grp, acts, (cast[w_out[0]], w_out[1]), x, mods["mix"][layer], W["norm_mlp"][layer],
                         mods["mlp"][layer], "out_proj_%d" % layer)
        yield from _await(cast, "mlp")
        w1b, w2b = cast["mlp"]
        last = layer + 1 == DEPTH
        res = _mlp(grp, h, w1b, w2b, layer, x, mods["mlp"][layer], None if last else W["norm_mix"][layer + 1],
                   None if last else mods["mix"][layer + 1], "mlp_%d" % layer)
        x = res[0]
        h = None if last else res[1]
    return unrow(x), {k: _stack(v) for k, v in outs.items()}


def _await(cast, key):
    while key not in cast:
        yield key


def _run_trunks(trunks):
    results = {}
    while trunks:
        for name in list(trunks):
            try:
                next(trunks[name])
            except StopIteration as done:
                results[name] = done.value
                del trunks[name]
    return results


def kernel(x_prompt, x_sample, cache_k, cache_v, state_lru_h, state_lru_conv, state_sconv, page_table, c_prompt, c_sample, norm_mix, norm_mlp, ada_mix_w, ada_mix_b, ada_mlp_w, ada_mlp_b, mlp_w1, mlp_w2, w_in_even, w_out_even, lru_conv_w, lru_conv_b, lru_wa, lru_ba, lru_wx, lru_bx, lru_lam, q_norm, k_norm, lambda_q1, lambda_k1, lambda_q2, lambda_k2, subln, w_in_odd, sconv_w, w_out_odd):
    bsz, seq, _ = x_prompt.shape
    dec_b, dec_seq, _ = x_sample.shape
    past_len = page_table.shape[1] * cache_k.shape[2]
    prompt = _Group(bsz, seq, time_major=False)
    sample = _Group(dec_b, dec_seq, time_major=True)

    n_c = bsz + dec_b
    pad = (-n_c) % (2 * SUBLANES)
    c_all = jnp.pad(jnp.concatenate([c_sample, c_prompt], axis=0), ((0, pad), (0, 0)))
    mods_p, mods_s = {}, {}
    for kind, w, b in (("mix", ada_mix_w, ada_mix_b), ("mlp", ada_mlp_w, ada_mlp_b)):
        m = _ada(c_all, w, b)
        mods_s[kind] = [m[l, :dec_b] for l in range(DEPTH)]
        mods_p[kind] = [m[l, dec_b:n_c].reshape(bsz, 1, 3 * D_MODEL) for l in range(DEPTH)]

    row = lambda a: a.reshape(a.shape[0], 1, a.shape[-1])
    W = {
        "norm_mix": row(norm_mix), "norm_mlp": row(norm_mlp),
        "mlp_w1": mlp_w1.reshape(DEPTH * D_MODEL, D_FF), "mlp_w2": mlp_w2.reshape(DEPTH * D_FF, D_MODEL),
        "w_in_even": w_in_even, "w_in_odd": w_in_odd,
        "w_out_even": w_out_even.reshape(-1, D_MODEL), "w_out_odd": w_out_odd.reshape(-1, D_MODEL),
        "sconv_w": sconv_w,
        "q_norm": jnp.tile(q_norm, (1, 2))[:, None, :], "k_norm": jnp.tile(k_norm, (1, 2))[:, None, :],
        "lambda_q1": lambda_q1, "lambda_k1": lambda_k1, "lambda_q2": lambda_q2, "lambda_k2": lambda_k2,
        "subln": subln,
        "lru": [(lru_conv_w[e], lru_conv_b[e][None, :], _block_diag(lru_wa[e]), lru_ba[e].reshape(1, W_LRU),
                 _block_diag(lru_wx[e]), lru_bx[e].reshape(1, W_LRU), lru_lam[e][None, :])
                for e in range(N_EVEN)],
    }

    cast = {}
    done = _run_trunks({
        "sample": _trunk(sample, _to_time_major(x_sample), mods_s, past_len + jnp.arange(dec_seq),
                         (state_lru_h, state_lru_conv, state_sconv), (cache_k, cache_v, page_table), W, cast),
        "prompt": _trunk(prompt, x_prompt.reshape(prompt.rows, D_MODEL), mods_p, jnp.arange(seq),
                         (None, None, None), None, W, cast),
    })
    (y_s, o_s), (y_p, o_p) = done["sample"], done["prompt"]
    return (y_p, y_s, o_p["k"], o_p["v"], o_s["k"], o_s["v"], o_p["h"], o_s["h"],
            o_p["cl"], o_s["cl"], o_p["sc"], o_s["sc"])
```

```python
import functools
import math

import jax
import jax.numpy as jnp
from jax import lax
from jax.experimental import pallas as pl
from jax.experimental.pallas import tpu as pltpu

D_MODEL = 2048
DEPTH = 2
PAGE_SIZE = 128
N_EVEN = (DEPTH + 1) // 2
N_ODD = DEPTH // 2
W_LRU = D_MODEL // 2
H_LRU = 16
BLK_LRU = W_LRU // H_LRU
CONV_LRU = 4
LRU_C = 8.0
H_ATT = 8
DV = (D_MODEL // 2) // H_ATT
DH = DV // 2
W_ATT = H_ATT * DV
W_SC = D_MODEL
SC_CONV = 3
D_FF = 4 * D_MODEL
ROPE_THETA = 10000.0
RMS_EPS = 1e-6
NEG_INF = -1e30

F32 = jnp.float32
BF16 = jnp.bfloat16

SUBLANES = 8
LANES = 128
MIB = 1 << 20
VMEM_LIMIT = 56 * MIB

TN_ADA = 2048
TM_PROJ = 1024
TM_OUT = 512
TF_MLP = 1024
TN_ODD = 512
TN_ODD_CAST = 256
QK_COLS = 256
TQ_ATT = 256
HEADS_PER_STEP = 2
T_LRU = 512
GATE_CHUNK = 256
EPILOGUE_ROWS = 16


def _params(sem):
    return pltpu.CompilerParams(dimension_semantics=sem, vmem_limit_bytes=VMEM_LIMIT)


def _dot(a, b):
    return jnp.dot(a, b, preferred_element_type=F32)


def _dot_nt(a, b):
    return lax.dot_general(a, b, (((1,), (1,)), ((), ())), preferred_element_type=F32)


def _modnorm(x, g, shift, scale):
    ms = jnp.mean(x * x, axis=-1, keepdims=True)
    y = x * lax.rsqrt(ms + RMS_EPS)
    return (y * g) * (1.0 + scale) + shift


def _mod_slabs(n_rows, mod_rows):
    out = []
    for r in range(0, n_rows, EPILOGUE_ROWS):
        m = slice(None) if mod_rows == 1 else slice(r % mod_rows, r % mod_rows + EPILOGUE_ROWS)
        out.append((slice(r, r + EPILOGUE_ROWS), m))
    return out


def _modnorm_rows(x_of, g, shift_ref, scale_ref, mrows):
    parts = []
    for lo in range(0, EPILOGUE_ROWS, SUBLANES):
        if isinstance(mrows, slice) and mrows.start is not None:
            m = slice(mrows.start + lo, mrows.start + lo + SUBLANES)
        else:
            m = mrows
        parts.append(_modnorm(x_of(lo, lo + SUBLANES), g, shift_ref[m, :], scale_ref[m, :]))
    return jnp.concatenate(parts, axis=0).astype(BF16)


def _ada_kernel(c_ref, w_ref, b_ref, o_ref):
    c = c_ref[...]
    a = (c * jax.nn.sigmoid(c)).astype(BF16)
    o_ref[...] = _dot(a, w_ref[...].astype(BF16)) + b_ref[...]


def _ada(c_all, w, b):
    rows = c_all.shape[0]
    tn = TN_ADA
    return pl.pallas_call(
        _ada_kernel,
        grid=(DEPTH, 3 * D_MODEL // tn),
        in_specs=[
            pl.BlockSpec((rows, D_MODEL), lambda l, j: (0, 0)),
            pl.BlockSpec((None, D_MODEL, tn), lambda l, j: (l, 0, j)),
            pl.BlockSpec((None, 1, tn), lambda l, j: (l, 0, j)),
        ],
        out_specs=pl.BlockSpec((None, rows, tn), lambda l, j: (l, 0, j)),
        out_shape=jax.ShapeDtypeStruct((DEPTH, rows, 3 * D_MODEL), F32),
        compiler_params=_params(("arbitrary", "arbitrary")),
        name="ada_modulation",
    )(c_all, w, b.reshape(DEPTH, 1, 3 * D_MODEL))


class _Group:
    def __init__(self, batch, seq, time_major):
        self.batch = batch
        self.seq = seq
        self.rows = batch * seq
        self.time_major = time_major

    def mod_spec(self, tm, part, row_axis=0):
        if self.time_major:
            assert tm % self.batch == 0
            return pl.BlockSpec((self.batch, D_MODEL), lambda *g: (0, part))
        tiles_per_batch = self.seq // tm
        return pl.BlockSpec((None, 1, D_MODEL), lambda *g: (g[row_axis] // tiles_per_batch, 0, part))


def _modnorm_kernel(x_ref, g_ref, shift_ref, scale_ref, h_ref):
    g = g_ref[...]
    for rows, mrows in _mod_slabs(x_ref.shape[0], shift_ref.shape[0]):
        x_of = lambda lo, hi, r0=rows.start: x_ref[r0 + lo:r0 + hi, :]
        h_ref[rows, :] = _modnorm_rows(x_of, g, shift_ref, scale_ref, mrows)


def _first_norm(grp, x, g, mod):
    tm = TM_OUT
    return pl.pallas_call(
        _modnorm_kernel,
        grid=(grp.rows // tm,),
        in_specs=[
            pl.BlockSpec((tm, D_MODEL), lambda i: (i, 0)),
            pl.BlockSpec((1, D_MODEL), lambda i: (0, 0)),
            grp.mod_spec(tm, 0),
            grp.mod_spec(tm, 1),
        ],
        out_specs=pl.BlockSpec((tm, D_MODEL), lambda i: (i, 0)),
        out_shape=jax.ShapeDtypeStruct((grp.rows, D_MODEL), BF16),
        compiler_params=_params(("arbitrary",)),
        name="first_norm",
    )(x, g, mod, mod)


def _weight_cols(w_ref, wb_ref, cols):
    w = w_ref[:, cols]
    if wb_ref is None:
        return w
    w = w.astype(BF16)
    wb_ref[:, cols] = w
    return w


def _qk_heads(h_ref, w_ref, wb_ref, g_ref, cos_ref, sin_ref):
    h = h_ref[...]
    g, cos, sin = g_ref[...], cos_ref[...], sin_ref[...]
    tm = h.shape[0]
    lane = lax.broadcasted_iota(jnp.int32, (tm, LANES), 1)
    first_map = lane < DH
    first_half = (lane % DH) < (DH // 2)
    out = []
    for c in range(W_ATT // QK_COLS):
        acc = _dot(h, _weight_cols(w_ref, wb_ref, slice(c * QK_COLS, (c + 1) * QK_COLS)))
        for hd in range(QK_COLS // LANES):
            xs = acc[:, hd * LANES:(hd + 1) * LANES]
            sq = xs * xs
            lo = jnp.sum(jnp.where(first_map, sq, 0.0), axis=-1, keepdims=True)
            hi = jnp.sum(jnp.where(first_map, 0.0, sq), axis=-1, keepdims=True)
            ms = jnp.where(first_map, lo, hi) * (1.0 / DH)
            y = xs * lax.rsqrt(ms + RMS_EPS) * g
            rot = jnp.where(first_half,
                            pltpu.roll(y, LANES - DH // 2, axis=1),
                            pltpu.roll(y, DH // 2, axis=1))
            out.append(y * cos + rot * sin)
    return out, first_map


def _q_prompt_kernel(h_ref, w_ref, g_ref, cos_ref, sin_ref, q1_ref, q2_ref):
    slabs, first_map = _qk_heads(h_ref, w_ref, None, g_ref, cos_ref, sin_ref)
    for hd, y in enumerate(slabs):
        y = y * (DH ** -0.5)
        cols = slice(hd * LANES, (hd + 1) * LANES)
        q1_ref[:, cols] = jnp.where(first_map, y, 0.0).astype(BF16)
        q2_ref[:, cols] = jnp.where(first_map, 0.0, y).astype(BF16)


def _k_prompt_kernel(h_ref, w_ref, g_ref, cos_ref, sin_ref, kf_ref, kb_ref):
    slabs, _ = _qk_heads(h_ref, w_ref, None, g_ref, cos_ref, sin_ref)
    for hd, y in enumerate(slabs):
        cols = slice(hd * LANES, (hd + 1) * LANES)
        kf_ref[:, cols] = y
        kb_ref[:, cols] = y.astype(BF16)


def _qk_sample_kernel(h_ref, w_ref, g_ref, cos_ref, sin_ref, o_ref, wb_ref, *, scale):
    slabs, _ = _qk_heads(h_ref, w_ref, wb_ref, g_ref, cos_ref, sin_ref)
    for hd, y in enumerate(slabs):
        o_ref[:, hd * LANES:(hd + 1) * LANES] = y * scale


def _v_prompt_kernel(h_ref, w_ref, vf_ref, vb_ref):
    acc = _dot(h_ref[...], w_ref[...])
    vf_ref[...] = acc
    vb_ref[...] = acc.astype(BF16)


def _plain_sample_kernel(h_ref, w_ref, o_ref, wb_ref):
    o_ref[...] = _dot(h_ref[...], _weight_cols(w_ref, wb_ref, slice(None)))


def _proj_call(kernel, grp, h, w, n_col_blocks, extra, extra_specs, out_dtypes, name):
    tm = min(TM_PROJ, grp.rows)
    tn = W_ATT
    out_spec = pl.BlockSpec((tm, tn), lambda i, j: (i, j))
    out_specs = [out_spec] * len(out_dtypes)
    out_shape = [jax.ShapeDtypeStruct((grp.rows, tn * n_col_blocks), dt) for dt in out_dtypes]
    w_cols = pl.BlockSpec((D_MODEL, tn), lambda i, j: (0, j))
    if isinstance(w, tuple):
        w, idx, col_block = w
        w_spec = pl.BlockSpec((None, D_MODEL, tn), lambda i, j: (idx, 0, col_block + j))
        out_specs.append(w_cols)
        out_shape.append(jax.ShapeDtypeStruct((D_MODEL, tn * n_col_blocks), BF16))
    else:
        w_spec = w_cols
    return pl.pallas_call(
        kernel,
        grid=(grp.rows // tm, n_col_blocks),
        in_specs=[pl.BlockSpec((tm, D_MODEL), lambda i, j: (i, 0)), w_spec] + extra_specs,
        out_specs=out_specs,
        out_shape=out_shape,
        compiler_params=_params(("arbitrary", "arbitrary")),
        name=name,
    )(h, w, *extra)


def _rope_specs(grp, cos):
    tm = min(TM_PROJ, grp.rows)
    tiles = cos.shape[0] // tm
    rope = pl.BlockSpec((tm, LANES), lambda i, j: (i % tiles, 0))
    return [pl.BlockSpec((1, LANES), lambda i, j: (0, 0)), rope, rope]


def _even_in_proj_sample(grp, h, w, e, qg, kg, cos, sin):
    specs = _rope_specs(grp, cos)
    q_kernel = functools.partial(_qk_sample_kernel, scale=DH ** -0.5)
    k_kernel = functools.partial(_qk_sample_kernel, scale=1.0)
    q, wq = _proj_call(q_kernel, grp, h, (w, e, 0), 1, (qg, cos, sin), specs, (F32,), "q_proj")
    kf, wk = _proj_call(k_kernel, grp, h, (w, e, 1), 1, (kg, cos, sin), specs, (F32,), "k_proj")
    vf, wv = _proj_call(_plain_sample_kernel, grp, h, (w, e, 2), 1, (), [], (F32,), "v_proj")
    xg, wl = _proj_call(_plain_sample_kernel, grp, h, (w, e, 3), 2, (), [], (F32,), "lru_proj")
    return q, kf, vf, xg, (wq, wk, wv, wl)


def _even_in_proj_prompt(grp, h, wq, wk, wv, qg, kg, cos, sin):
    specs = _rope_specs(grp, cos)
    q1, q2 = _proj_call(_q_prompt_kernel, grp, h, wq, 1, (qg, cos, sin), specs, (BF16, BF16), "q_proj")
    kf, kb = _proj_call(_k_prompt_kernel, grp, h, wk, 1, (kg, cos, sin), specs, (F32, BF16), "k_proj")
    vf, vb = _proj_call(_v_prompt_kernel, grp, h, wv, 1, (), [], (F32, BF16), "v_proj")
    return q1, q2, kf, kb, vf, vb


def _diff_lambda(lq1_ref, lk1_ref, lq2_ref, lk2_ref, lam_init):
    a = jnp.sum(lq1_ref[...] * lk1_ref[...], axis=-1, keepdims=True)
    b = jnp.sum(lq2_ref[...] * lk2_ref[...], axis=-1, keepdims=True)
    return jnp.exp(a) - jnp.exp(b) + lam_init


def _sub_norm(o, g, lam_init):
    ms = jnp.mean(o * o, axis=-1, keepdims=True)
    return (o * lax.rsqrt(ms + RMS_EPS) * g) * (1.0 - lam_init)


def _attn_prompt_kernel(lq1_ref, lk1_ref, lq2_ref, lk2_ref, g_ref, q1_ref, q2_ref, k_ref, v_ref, *cast_refs,
                        tq, n_tiles, lam_init):
    n_cast = len(cast_refs) // 2
    o_ref = cast_refs[n_cast]
    qi = pl.program_id(2)
    lam = _diff_lambda(lq1_ref, lk1_ref, lq2_ref, lk2_ref, lam_init)
    row = lax.broadcasted_iota(jnp.int32, (tq, tq), 0)
    col = lax.broadcasted_iota(jnp.int32, (tq, tq), 1)
    causal = col <= row

    def body(c):
        for src_ref, dst_ref in zip(cast_refs[:n_cast], cast_refs[n_cast + 1:]):
            if len(dst_ref.shape) == 2:
                dst_ref[...] = src_ref[...].astype(BF16)
            else:
                tile = dst_ref.shape[2]
                for j in range(dst_ref.shape[0]):
                    dst_ref[j] = src_ref[:, j * tile:(j + 1) * tile].astype(BF16)
        n_past = c * tq
        for hd in range(o_ref.shape[1] // DV):
            cols = slice(hd * DV, (hd + 1) * DV)
            kd = k_ref[n_past:n_past + tq, cols]
            vd = v_ref[n_past:n_past + tq, cols]
            maps = []
            for q_ref in (q1_ref, q2_ref):
                q = q_ref[:, cols]
                sd = jnp.where(causal, _dot_nt(q, kd), NEG_INF)
                m = jnp.max(sd, axis=-1, keepdims=True)
                if c > 0:
                    sp = _dot_nt(q, k_ref[0:n_past, cols])
                    m = jnp.maximum(m, jnp.max(sp, axis=-1, keepdims=True))
                pd = jnp.exp(sd - m)
                l = jnp.sum(pd, axis=-1, keepdims=True)
                acc = _dot(pd.astype(BF16), vd)
                if c > 0:
                    pp = jnp.exp(sp - m)
                    l = l + jnp.sum(pp, axis=-1, keepdims=True)
                    acc = acc + _dot(pp.astype(BF16), v_ref[0:n_past, cols])
                maps.append(acc / l)
            o = maps[0] - lam * maps[1]
            o_ref[:, cols] = _sub_norm(o, g_ref[...], lam_init).astype(BF16)

    for c in range(n_tiles):
        pl.when(qi == c)(functools.partial(body, c))


def _attn_prompt(grp, q1, q2, kb, vb, lam_vecs, subln, lam_init, to_cast):
    tq = TQ_ATT
    tiles = grp.seq // tq
    n_groups = H_ATT // HEADS_PER_STEP
    width = HEADS_PER_STEP * DV
    steps = grp.batch * n_groups * tiles
    vec64 = pl.BlockSpec((1, DH), lambda b, h, i: (0, 0))
    q_spec = pl.BlockSpec((tq, width), lambda b, h, i: (b * tiles + i, h))
    kv_spec = pl.BlockSpec((grp.seq, width), lambda b, h, i: (b, h))
    step = lambda b, h, i: (b * n_groups + h) * tiles + i
    src_specs, dst_specs, dst_shapes = [], [], []
    for a, tile in to_cast:
        rows, cols = a.shape
        assert rows % (steps * 2 * SUBLANES) == 0
        src_specs.append(pl.BlockSpec((rows // steps, cols), lambda b, h, i: (step(b, h, i), 0)))
        if tile is None:
            dst_specs.append(src_specs[-1])
            dst_shapes.append(jax.ShapeDtypeStruct((rows, cols), BF16))
        else:
            dst_specs.append(pl.BlockSpec((cols // tile, rows // steps, tile), lambda b, h, i: (0, step(b, h, i), 0)))
            dst_shapes.append(jax.ShapeDtypeStruct((cols // tile, rows, tile), BF16))
    res = pl.pallas_call(
        functools.partial(_attn_prompt_kernel, tq=tq, n_tiles=tiles, lam_init=lam_init),
        grid=(grp.batch, n_groups, tiles),
        in_specs=[vec64] * 4 + [pl.BlockSpec((1, DV), lambda b, h, i: (0, 0)), q_spec, q_spec, kv_spec, kv_spec]
        + src_specs,
        out_specs=[q_spec] + dst_specs,
        out_shape=[jax.ShapeDtypeStruct((grp.rows, W_ATT), BF16)] + dst_shapes,
        compiler_params=_params(("arbitrary", "arbitrary", "arbitrary")),
        name="attn_prompt",
    )(*lam_vecs, subln, q1, q2, kb, vb, *[a for a, _ in to_cast])
    return res[0], res[1:]


def _attn_sample_kernel(pt_ref, lq1_ref, lk1_ref, lq2_ref, lk2_ref, g_ref, q_ref, kn_ref, vn_ref, *rest,
                        n_pages, lam_init):
    k_refs = rest[:n_pages]
    v_refs = rest[n_pages:2 * n_pages]
    o_ref = rest[2 * n_pages]
    del pt_ref
    n_q = q_ref.shape[0]
    half = H_ATT // 2
    per_head = 2 * n_q
    n_past = n_pages * PAGE_SIZE
    lam = _diff_lambda(lq1_ref, lk1_ref, lq2_ref, lk2_ref, lam_init)
    g = g_ref[...]
    lane = lax.broadcasted_iota(jnp.int32, (n_q, LANES), 1)
    row = lax.broadcasted_iota(jnp.int32, (2 * per_head, 1), 0)
    t_of_row = row % n_q
    second = row >= per_head
    col = lax.broadcasted_iota(jnp.int32, (2 * per_head, 2 * n_past), 1)
    own_head = (col % 2) == (lax.broadcasted_iota(jnp.int32, (2 * per_head, 2 * n_past), 0) // per_head)
    for hp in range(half):
        heads = (hp, hp + half)
        qbd = []
        for hd in heads:
            q = q_ref[:, hd, :]
            qbd += [jnp.where(lane < DH, q, 0.0), jnp.where(lane < DH, 0.0, q)]
        qbd = jnp.concatenate(qbd, axis=0)
        pair_rows = pl.ds(hp, 2 * PAGE_SIZE, stride=half)
        kp = jnp.concatenate([r[pair_rows, :].astype(BF16) for r in k_refs], axis=0)
        vp = jnp.concatenate([r[pair_rows, :].astype(BF16) for r in v_refs], axis=0)
        s = jnp.where(own_head, _dot_nt(qbd.astype(BF16), kp), NEG_INF)
        kn = [kn_ref[:, hd, :] for hd in heads]
        vn = [vn_ref[:, hd, :] for hd in heads]
        s_new = []
        for j in range(n_q):
            kj = jnp.where(second, kn[1][j:j + 1, :], kn[0][j:j + 1, :])
            s_new.append(jnp.where(t_of_row >= j, jnp.sum(qbd * kj, axis=-1, keepdims=True), NEG_INF))
        m = jnp.max(s, axis=-1, keepdims=True)
        for sj in s_new:
            m = jnp.maximum(m, sj)
        p = jnp.exp(s - m)
        l = jnp.sum(p, axis=-1, keepdims=True)
        acc = _dot(p.astype(BF16), vp)
        for j, sj in enumerate(s_new):
            pj = jnp.exp(sj - m)
            l = l + pj
            acc = acc + pj * jnp.where(second, vn[1][j:j + 1, :], vn[0][j:j + 1, :])
        o = acc / l
        for a, hd in enumerate(heads):
            oa = o[a * per_head:a * per_head + n_q] - lam * o[a * per_head + n_q:(a + 1) * per_head]
            o_ref[:, hd, :] = _sub_norm(oa, g, lam_init)


def _attn_sample(grp, e, q, kf, vf, cache_k, cache_v, page_table, lam_vecs, subln, lam_init):
    n_pages = page_table.shape[1]
    tok = (grp.seq, grp.batch, H_ATT, LANES)
    vec64 = pl.BlockSpec((1, DH), lambda b, pt: (0, 0))
    new_spec = pl.BlockSpec((grp.seq, None, H_ATT, LANES), lambda b, pt: (0, b, 0, 0))

    def page_spec(p):
        return pl.BlockSpec((None, None, PAGE_SIZE * H_ATT, LANES),
                            lambda b, pt: (e, pt[b * n_pages + p], 0, 0))

    pages = [page_spec(p) for p in range(n_pages)]
    page_rows = cache_k.shape[:2] + (PAGE_SIZE * H_ATT, LANES)
    cache_k = cache_k.reshape(page_rows)
    cache_v = cache_v.reshape(page_rows)
    out = pl.pallas_call(
        functools.partial(_attn_sample_kernel, n_pages=n_pages, lam_init=lam_init),
        grid_spec=pltpu.PrefetchScalarGridSpec(
            num_scalar_prefetch=1,
            grid=(grp.batch,),
            in_specs=[vec64] * 4 + [pl.BlockSpec((1, DV), lambda b, pt: (0, 0)), new_spec, new_spec, new_spec]
            + pages + pages,
            out_specs=new_spec,
        ),
        out_shape=jax.ShapeDtypeStruct(tok, F32),
        compiler_params=_params(("arbitrary",)),
        name="attn_sample",
    )(page_table.reshape(-1), *lam_vecs, subln, q.reshape(tok), kf.reshape(tok), vf.reshape(tok),
      *([cache_k] * n_pages), *([cache_v] * n_pages))
    return out.reshape(grp.rows, W_ATT)


def _shifted(u, hist, s):
    row = lax.broadcasted_iota(jnp.int32, u.shape, 0)
    return jnp.where(row < s, pltpu.roll(hist, s, axis=0), pltpu.roll(u, s, axis=0))


def _scan_rows(a, b, carry):
    r, w = a.shape
    groups = r // SUBLANES
    a = a.reshape(groups, SUBLANES, w)
    b = b.reshape(groups, SUBLANES, w)
    sub = lax.broadcasted_iota(jnp.int32, a.shape, 1)
    s = 1
    while s < SUBLANES:
        valid = sub >= s
        b = jnp.where(valid, a * pltpu.roll(b, s, axis=1) + b, b)
        a = jnp.where(valid, a * pltpu.roll(a, s, axis=1), a)
        s *= 2
    hs = []
    for grp_i in range(groups):
        hg = a[grp_i] * carry + b[grp_i]
        hs.append(hg)
        carry = hg[SUBLANES - 1:, :]
    return hs


def _lru_gates(xc, wa, ba, wx, bx, lam):
    xb = xc.astype(BF16)
    r_parts, i_parts = [], []
    for c, (wa_c, wx_c) in enumerate(zip(wa, wx)):
        cols = slice(c * GATE_CHUNK, (c + 1) * GATE_CHUNK)
        r_parts.append(_dot(xb[:, cols], wa_c))
        i_parts.append(_dot(xb[:, cols], wx_c))
    cat = (lambda parts: parts[0] if len(parts) == 1 else jnp.concatenate(parts, axis=1))
    r = jax.nn.sigmoid(cat(r_parts) + ba)
    i = jax.nn.sigmoid(cat(i_parts) + bx)
    log_a = (-LRU_C) * r * jax.nn.softplus(-lam)
    a = jnp.exp(log_a)
    drive = jnp.sqrt(1.0 - a * a) * (i * xc)
    return a, drive


def _lru_prompt_kernel(h_ref, wl_ref, cw_ref, cb_ref, wa_ref, ba_ref, wx_ref, bx_ref, lam_ref,
                       y_ref, hlast_ref, ctail_ref, hist_ref, hprev_ref):
    t = pl.program_id(1)
    rows = h_ref.shape[0]

    @pl.when(t == 0)
    def _():
        hist_ref[...] = jnp.zeros_like(hist_ref)
        hprev_ref[...] = jnp.zeros_like(hprev_ref)

    hin = h_ref[...]
    xl = _dot(hin, wl_ref[:, :W_LRU])
    gl = _dot(hin, wl_ref[:, W_LRU:])
    hist = hist_ref[...]
    xc = xl * cw_ref[CONV_LRU - 1:CONV_LRU, :] + cb_ref[...]
    for s in range(1, CONV_LRU):
        xc = xc + _shifted(xl, hist, s) * cw_ref[CONV_LRU - 1 - s:CONV_LRU - s, :]
    n_chunks = W_LRU // GATE_CHUNK
    a, drive = _lru_gates(xc, [wa_ref[c] for c in range(n_chunks)], ba_ref[...],
                          [wx_ref[c] for c in range(n_chunks)], bx_ref[...], lam_ref[...])
    hs = _scan_rows(a, drive, hprev_ref[SUBLANES - 1:, :])
    y_ref[...] = (jax.nn.gelu(gl) * jnp.concatenate(hs, axis=0)).astype(BF16)
    hist_ref[...] = xl
    hprev_ref[...] = hs[-1]
    hlast_ref[...] = hs[-1]
    ctail_ref[...] = xl[rows - SUBLANES:, :]


def _lru_prompt(grp, h, wl, lru_w):
    tt = T_LRU
    tiles = grp.seq // tt
    n_chunks = W_LRU // GATE_CHUNK

    def full(shape):
        return pl.BlockSpec(shape, lambda b, t: (0,) * len(shape))

    tail = pl.BlockSpec((None, SUBLANES, W_LRU), lambda b, t: (b, 0, 0))
    tail_shape = jax.ShapeDtypeStruct((grp.batch, SUBLANES, W_LRU), F32)
    y, hlast, ctail = pl.pallas_call(
        _lru_prompt_kernel,
        grid=(grp.batch, tiles),
        in_specs=[pl.BlockSpec((tt, D_MODEL), lambda b, t: (b * tiles + t, 0)),
                  full((D_MODEL, 2 * W_LRU)),
                  full((CONV_LRU, W_LRU)), full((1, W_LRU)), full((n_chunks, GATE_CHUNK, GATE_CHUNK)),
                  full((1, W_LRU)), full((n_chunks, GATE_CHUNK, GATE_CHUNK)), full((1, W_LRU)), full((1, W_LRU))],
        out_specs=[pl.BlockSpec((tt, W_LRU), lambda b, t: (b * tiles + t, 0)), tail, tail],
        out_shape=[jax.ShapeDtypeStruct((grp.rows, W_LRU), BF16), tail_shape, tail_shape],
        scratch_shapes=[pltpu.VMEM((tt, W_LRU), F32), pltpu.VMEM((SUBLANES, W_LRU), F32)],
        compiler_params=_params(("arbitrary", "arbitrary")),
        name="lru_prompt",
    )(h, wl, *lru_w)
    return y, hlast[:, SUBLANES - 1, :], ctail[:, SUBLANES - (CONV_LRU - 1):, :]


def _lru_sample_kernel(xl_ref, gl_ref, hist_ref, h0_ref, cw_ref, cb_ref, wa_ref, ba_ref, wx_ref, bx_ref, lam_ref,
                       y_ref, h_ref, *, n_t):
    nb = h0_ref.shape[0]

    def slab(t):
        return slice(t * nb, (t + 1) * nb)

    def u(t):
        return xl_ref[slab(t), :] if t >= 0 else hist_ref[slab(CONV_LRU - 1 + t), :]

    xc = []
    for t in range(n_t):
        acc = u(t) * cw_ref[CONV_LRU - 1:CONV_LRU, :] + cb_ref[...]
        for s in range(1, CONV_LRU):
            acc = acc + u(t - s) * cw_ref[CONV_LRU - 1 - s:CONV_LRU - s, :]
        xc.append(acc)
    xc = jnp.concatenate(xc, axis=0)
    a, drive = _lru_gates(xc, [wa_ref[0]], ba_ref[...], [wx_ref[0]], bx_ref[...], lam_ref[...])
    h = h0_ref[...]
    for t in range(n_t):
        h = a[slab(t), :] * h + drive[slab(t), :]
        y_ref[slab(t), :] = (jax.nn.gelu(gl_ref[slab(t), :]) * h).astype(BF16)
    h_ref[...] = h


def _lru_sample(grp, xg, hist, h0, lru_w):
    gc = GATE_CHUNK
    n_chunks = W_LRU // gc
    vec = pl.BlockSpec((1, gc), lambda c: (0, c))
    gate_w = pl.BlockSpec((1, gc, gc), lambda c: (c, 0, 0))
    return pl.pallas_call(
        functools.partial(_lru_sample_kernel, n_t=grp.seq),
        grid=(n_chunks,),
        in_specs=[pl.BlockSpec((grp.rows, gc), lambda c: (0, c)),
                  pl.BlockSpec((grp.rows, gc), lambda c: (0, n_chunks + c)),
                  pl.BlockSpec((hist.shape[0], gc), lambda c: (0, c)),
                  pl.BlockSpec((grp.batch, gc), lambda c: (0, c)),
                  pl.BlockSpec((CONV_LRU, gc), lambda c: (0, c)), vec, gate_w, vec, gate_w, vec, vec],
        out_specs=[pl.BlockSpec((grp.rows, gc), lambda c: (0, c)), pl.BlockSpec((grp.batch, gc), lambda c: (0, c))],
        out_shape=[jax.ShapeDtypeStruct((grp.rows, W_LRU), BF16), jax.ShapeDtypeStruct((grp.batch, W_LRU), F32)],
        compiler_params=_params(("arbitrary",)),
        name="lru_sample",
    )(xg, xg, hist, h0, *lru_w)


def _residual_epilogue(acc, x_ref, gate_ref, norm, x_out_ref, h_out_ref):
    g = None if norm is None else norm[0][...]
    for rows, mrows in _mod_slabs(x_ref.shape[0], gate_ref.shape[0]):
        r0 = rows.start

        def x_new_of(lo, hi, r0=r0, mrows=mrows):
            m = mrows if mrows.start is None else slice(mrows.start + lo, mrows.start + hi)
            x_new = x_ref[r0 + lo:r0 + hi, :] + gate_ref[m, :] * acc[r0 + lo:r0 + hi, :]
            x_out_ref[r0 + lo:r0 + hi, :] = x_new
            return x_new

        if h_out_ref is not None:
            h_out_ref[rows, :] = _modnorm_rows(x_new_of, g, norm[1], norm[2], mrows)
        else:
            for lo in range(0, EPILOGUE_ROWS, SUBLANES):
                x_new_of(lo, lo + SUBLANES)


def _out_proj_kernel(*refs, n_in):
    a_refs = refs[:n_in]
    w_ref, x_ref, gate_ref, g_ref, shift_ref, scale_ref, x_out_ref, h_out_ref = refs[n_in:]
    acc = None
    k0 = 0
    for a_ref in a_refs:
        k = a_ref.shape[1]
        part = _dot(a_ref[...].astype(BF16), w_ref[k0:k0 + k, :])
        acc = part if acc is None else acc + part
        k0 += k
    _residual_epilogue(acc, x_ref, gate_ref, (g_ref, shift_ref, scale_ref), x_out_ref, h_out_ref)


def _out_proj(grp, acts, w, x, gate_mod, g_next, next_mod, name):
    w, idx = w
    tm = TM_OUT
    row = pl.BlockSpec((tm, D_MODEL), lambda i: (i, 0))
    return pl.pallas_call(
        functools.partial(_out_proj_kernel, n_in=len(acts)),
        grid=(grp.rows // tm,),
        in_specs=[pl.BlockSpec((tm, a.shape[1]), lambda i: (i, 0)) for a in acts] + [
            pl.BlockSpec((D_MODEL, D_MODEL), lambda i: (idx, 0)),
            row,
            grp.mod_spec(tm, 2),
            pl.BlockSpec((1, D_MODEL), lambda i: (0, 0)),
            grp.mod_spec(tm, 0),
            grp.mod_spec(tm, 1),
        ],
        out_specs=[row, row],
        out_shape=[jax.ShapeDtypeStruct((grp.rows, D_MODEL), F32), jax.ShapeDtypeStruct((grp.rows, D_MODEL), BF16)],
        compiler_params=_params(("arbitrary",)),
        name=name,
    )(*acts, w, x, gate_mod, g_next, next_mod, next_mod)


def _mlp_kernel(*refs, with_norm):
    if with_norm:
        h_ref, w1_ref, w2_ref, x_ref, gate_ref, g_ref, shift_ref, scale_ref, x_out_ref, h_out_ref = refs
        norm = (g_ref, shift_ref, scale_ref)
    else:
        h_ref, w1_ref, w2_ref, x_ref, gate_ref, x_out_ref = refs
        norm, h_out_ref = None, None
    f = pl.program_id(1)

    @pl.when(f == 0)
    def _():
        x_out_ref[...] = jnp.zeros_like(x_out_ref)

    hid = jnp.maximum(_dot(h_ref[...], w1_ref[...]), 0.0)
    x_out_ref[...] += _dot((hid * hid).astype(BF16), w2_ref[...])

    @pl.when(f == pl.num_programs(1) - 1)
    def _():
        _residual_epilogue(x_out_ref, x_ref, gate_ref, norm, x_out_ref, h_out_ref)


def _mlp(grp, h, w1, w2, layer, x, gate_mod, g_next, next_mod, name):
    tm, tf = TM_OUT, TF_MLP
    with_norm = g_next is not None
    row = pl.BlockSpec((tm, D_MODEL), lambda i, f: (i, 0))
    n_f = D_FF // tf
    in_specs = [row,
                pl.BlockSpec((None, D_MODEL, tf), lambda i, f: (f, layer, 0)),
                pl.BlockSpec((tf, D_MODEL), lambda i, f: (layer * n_f + f, 0)),
                row,
                grp.mod_spec(tm, 2)]
    args = [h, w1, w2, x, gate_mod]
    out_specs = [row]
    out_shape = [jax.ShapeDtypeStruct((grp.rows, D_MODEL), F32)]
    if with_norm:
        in_specs += [pl.BlockSpec((1, D_MODEL), lambda i, f: (0, 0)), grp.mod_spec(tm, 0), grp.mod_spec(tm, 1)]
        args += [g_next, next_mod, next_mod]
        out_specs.append(row)
        out_shape.append(jax.ShapeDtypeStruct((grp.rows, D_MODEL), BF16))
    return pl.pallas_call(
        functools.partial(_mlp_kernel, with_norm=with_norm),
        grid=(grp.rows // tm, D_FF // tf),
        in_specs=in_specs,
        out_specs=out_specs,
        out_shape=out_shape,
        compiler_params=_params(("arbitrary", "arbitrary")),
        name=name,
    )(*args)


def _sconv_prompt_kernel(h_ref, wb_ref, wc_ref, wx_ref, cw_ref, g_ref, tail_ref, hist_ref, *, tiles_per_batch):
    i = pl.program_id(1)
    rows = h_ref.shape[0]

    @pl.when(i % tiles_per_batch == 0)
    def _():
        hist_ref[...] = jnp.zeros_like(hist_ref)

    h = h_ref[...]
    u = _dot(h, wc_ref[...]) * _dot(h, wx_ref[...])
    hist = hist_ref[...]
    conv = u * cw_ref[SC_CONV - 1:SC_CONV, :]
    for s in range(1, SC_CONV):
        conv = conv + _shifted(u, hist, s) * cw_ref[SC_CONV - 1 - s:SC_CONV - s, :]
    g_ref[...] = (_dot(h, wb_ref[...]) * conv).astype(BF16)
    hist_ref[...] = u
    tail_ref[...] = u[rows - SUBLANES:, :]


def _sconv_sample_kernel(h_ref, wb_ref, wc_ref, wx_ref, cw_ref, hist_ref, g_ref, tail_ref,
                         wb_out_ref, wc_out_ref, wx_out_ref, *, n_t):
    h = h_ref[...]
    every = slice(None)
    u = _dot(h, _weight_cols(wc_ref, wc_out_ref, every)) * _dot(h, _weight_cols(wx_ref, wx_out_ref, every))
    b = _dot(h, _weight_cols(wb_ref, wb_out_ref, every))
    nb = h.shape[0] // n_t

    def slab(t):
        return slice(t * nb, (t + 1) * nb)

    def ut(t):
        return u[slab(t), :] if t >= 0 else hist_ref[slab(SC_CONV - 1 + t), :]

    for t in range(n_t):
        conv = ut(t) * cw_ref[SC_CONV - 1:SC_CONV, :]
        for s in range(1, SC_CONV):
            conv = conv + ut(t - s) * cw_ref[SC_CONV - 1 - s:SC_CONV - s, :]
        g_ref[slab(t), :] = (b[slab(t), :] * conv).astype(BF16)
    tail_ref[...] = u[(n_t - (SC_CONV - 1)) * nb:, :]


def _sconv_prompt(grp, h, w_parts, conv_w):
    tm, tn = TM_PROJ, TN_ODD
    tiles_per_batch = grp.seq // tm
    w_cols = pl.BlockSpec((D_MODEL, tn), lambda n, i: (0, n))
    g, tail = pl.pallas_call(
        functools.partial(_sconv_prompt_kernel, tiles_per_batch=tiles_per_batch),
        grid=(W_SC // tn, grp.rows // tm),
        in_specs=[pl.BlockSpec((tm, D_MODEL), lambda n, i: (i, 0)), w_cols, w_cols, w_cols,
                  pl.BlockSpec((SC_CONV, tn), lambda n, i: (0, n))],
        out_specs=[pl.BlockSpec((tm, tn), lambda n, i: (i, n)),
                   pl.BlockSpec((None, SUBLANES, tn), lambda n, i: (i // tiles_per_batch, 0, n))],
        out_shape=[jax.ShapeDtypeStruct((grp.rows, W_SC), BF16),
                   jax.ShapeDtypeStruct((grp.batch, SUBLANES, W_SC), F32)],
        scratch_shapes=[pltpu.VMEM((tm, tn), F32)],
        compiler_params=_params(("arbitrary", "arbitrary")),
        name="sconv_prompt",
    )(h, *w_parts, conv_w)
    return g, tail[:, SUBLANES - (SC_CONV - 1):, :]


def _sconv_sample(grp, h, w_in, idx, conv_w, hist):
    tm, tn = grp.rows, TN_ODD_CAST
    n_blocks = W_SC // tn
    n_tail = (SC_CONV - 1) * grp.batch
    cols = pl.BlockSpec((D_MODEL, tn), lambda n: (0, n))
    w_shape = jax.ShapeDtypeStruct((D_MODEL, W_SC), BF16)
    g, tail, wb, wc, wx = pl.pallas_call(
        functools.partial(_sconv_sample_kernel, n_t=grp.seq),
        grid=(n_blocks,),
        in_specs=[pl.BlockSpec((tm, D_MODEL), lambda n: (0, 0)),
                  pl.BlockSpec((None, D_MODEL, tn), lambda n: (idx, 0, n)),
                  pl.BlockSpec((None, D_MODEL, tn), lambda n: (idx, 0, n_blocks + n)),
                  pl.BlockSpec((None, D_MODEL, tn), lambda n: (idx, 0, 2 * n_blocks + n)),
                  pl.BlockSpec((SC_CONV, tn), lambda n: (0, n)),
                  pl.BlockSpec((n_tail, tn), lambda n: (0, n))],
        out_specs=[pl.BlockSpec((tm, tn), lambda n: (0, n)), pl.BlockSpec((n_tail, tn), lambda n: (0, n)),
                   cols, cols, cols],
        out_shape=[jax.ShapeDtypeStruct((grp.rows, W_SC), BF16), jax.ShapeDtypeStruct((n_tail, W_SC), F32),
                   w_shape, w_shape, w_shape],
        compiler_params=_params(("arbitrary",)),
        name="sconv_sample",
    )(h, w_in, w_in, w_in, conv_w, hist)
    return g, tail, (wb, wc, wx)


def _rope_tables(pos):
    half = DH // 2
    inv = ROPE_THETA ** (-jnp.arange(half, dtype=F32) / half)
    ang = pos.astype(F32)[:, None] * inv[None, :]
    cos = jnp.cos(ang)
    sin = jnp.sin(ang)
    cos = jnp.concatenate([cos, cos, cos, cos], axis=-1)
    sin = jnp.concatenate([-sin, sin, -sin, sin], axis=-1)
    return cos, sin


def _block_diag(w):
    per = GATE_CHUNK // BLK_LRU
    w = w.reshape(W_LRU // GATE_CHUNK, per, BLK_LRU, BLK_LRU)
    eye = jnp.eye(per, dtype=w.dtype)
    out = jnp.einsum("cpij,pq->cpiqj", w, eye)
    return out.reshape(W_LRU // GATE_CHUNK, GATE_CHUNK, GATE_CHUNK).astype(BF16)


def _to_time_major(a):
    a = jnp.swapaxes(a, 0, 1)
    return a.reshape((a.shape[0] * a.shape[1],) + a.shape[2:])


def _from_time_major(a, batch):
    return jnp.swapaxes(a.reshape((a.shape[0] // batch, batch) + a.shape[1:]), 0, 1)


def _stack(parts):
    return parts[0][None] if len(parts) == 1 else jnp.stack(parts)


def _trunk(grp, x, mods, pos, state, kv_cache, W, cast):
    lru_h0, lru_hist, sconv_hist = state
    outs = {n: [] for n in ("k", "v", "h", "cl", "sc")}
    unrow = (lambda a: _from_time_major(a, grp.batch)) if grp.time_major else (
        lambda a: a.reshape((grp.batch, a.shape[0] // grp.batch) + a.shape[1:]))
    h = _first_norm(grp, x, W["norm_mix"][0], mods["mix"][0])
    for layer in range(DEPTH):
        if layer % 2 == 0:
            e = layer // 2
            lam_init = 0.8 - 0.6 * math.exp(-0.3 * layer)
            cos, sin = _rope_tables(pos)
            lam_vecs = [W[n][e][None, :] for n in ("lambda_q1", "lambda_k1", "lambda_q2", "lambda_k2")]
            subln = W["subln"][e][None, :]
            if grp.time_major:
                cos, sin = jnp.repeat(cos, grp.batch, axis=0), jnp.repeat(sin, grp.batch, axis=0)
                q, kf, vf, xg, cast["in_even", e] = _even_in_proj_sample(
                    grp, h, W["w_in_even"], e, W["q_norm"][e], W["k_norm"][e], cos, sin)
                cache_k, cache_v, page_table = kv_cache
                o = _attn_sample(grp, e, q, kf, vf, cache_k, cache_v, page_table, lam_vecs, subln, lam_init)
                yl, h_last = _lru_sample(grp, xg, _to_time_major(lru_hist[e]), lru_h0[e], W["lru"][e])
                conv_tail = unrow(xg[(grp.seq - (CONV_LRU - 1)) * grp.batch:, :W_LRU])
            else:
                yield from _await(cast, ("in_even", e))
                wq, wk, wv, wl = cast["in_even", e]
                q1, q2, kf, kb, vf, vb = _even_in_proj_prompt(grp, h, wq, wk, wv, W["q_norm"][e], W["k_norm"][e],
                                                              cos, sin)
                to_cast = () if "mlp" in cast else ((W["mlp_w1"], TF_MLP), (W["mlp_w2"], None),
                                                    (W["w_out_even"], None), (W["w_out_odd"], None))
                o, casted = _attn_prompt(grp, q1, q2, kb, vb, lam_vecs, subln, lam_init, to_cast)
                if to_cast:
                    cast["mlp"] = casted[:2]
                    cast["w_out_even"], cast["w_out_odd"] = casted[2:]
                yl, h_last, conv_tail = _lru_prompt(grp, h, wl, W["lru"][e])
            outs["k"].append(unrow(kf).reshape(grp.batch, grp.seq, H_ATT, 2 * DH))
            outs["v"].append(unrow(vf).reshape(grp.batch, grp.seq, H_ATT, DV))
            outs["h"].append(h_last)
            outs["cl"].append(conv_tail)
            acts, w_out = [o, yl], ("w_out_even", e)
        else:
            o_idx = layer // 2
            if grp.time_major:
                g, sc, cast["in_odd", o_idx] = _sconv_sample(grp, h, W["w_in_odd"], o_idx, W["sconv_w"][o_idx],
                                                             _to_time_major(sconv_hist[o_idx]))
                sc = unrow(sc)
            else:
                yield from _await(cast, ("in_odd", o_idx))
                g, sc = _sconv_prompt(grp, h, cast["in_odd", o_idx], W["sconv_w"][o_idx])
            outs["sc"].append(sc)
            acts, w_out = [g], ("w_out_odd", o_idx)
        yield from _await(cast, w_out[0])
        x, h = _out_proj(grp, acts, (cast[w_out[0]], w_out[1]), x, mods["mix"][layer], W["norm_mlp"][layer],
                         mods["mlp"][layer], "out_proj_%d" % layer)
        yield from _await(cast, "mlp")
        w1b, w2b = cast["mlp"]
        last = layer + 1 == DEPTH
        res = _mlp(grp, h, w1b, w2b, layer, x, mods["mlp"][layer], None if last else W["norm_mix"][layer + 1],
                   None if last else mods["mix"][layer + 1], "mlp_%d" % layer)
        x = res[0]
        h = None if last else res[1]
    return unrow(x), {k: _stack(v) for k, v in outs.items()}


def _await(cast, key):
    while key not in cast:
        yield key


def _run_trunks(trunks):
    results = {}
    while trunks:
        for name in list(trunks):
            try:
                next(trunks[name])
            except StopIteration as done:
                results[name] = done.value
                del trunks[name]
    return results


def kernel(x_prompt, x_sample, cache_k, cache_v, state_lru_h, state_lru_conv, state_sconv, page_table, c_prompt, c_sample, norm_mix, norm_mlp, ada_mix_w, ada_mix_b, ada_mlp_w, ada_mlp_b, mlp_w1, mlp_w2, w_in_even, w_out_even, lru_conv_w, lru_conv_b, lru_wa, lru_ba, lru_wx, lru_bx, lru_lam, q_norm, k_norm, lambda_q1, lambda_k1, lambda_q2, lambda_k2, subln, w_in_odd, sconv_w, w_out_odd):
    bsz, seq, _ = x_prompt.shape
    dec_b, dec_seq, _ = x_sample.shape
    past_len = page_table.shape[1] * cache_k.shape[2]
    prompt = _Group(bsz, seq, time_major=False)
    sample = _Group(dec_b, dec_seq, time_major=True)

    n_c = bsz + dec_b
    pad = (-n_c) % (2 * SUBLANES)
    c_all = jnp.pad(jnp.concatenate([c_sample, c_prompt], axis=0), ((0, pad), (0, 0)))
    mods_p, mods_s = {}, {}
    for kind, w, b in (("mix", ada_mix_w, ada_mix_b), ("mlp", ada_mlp_w, ada_mlp_b)):
        m = _ada(c_all, w, b)
        mods_s[kind] = [m[l, :dec_b] for l in range(DEPTH)]
        mods_p[kind] = [m[l, dec_b:n_c].reshape(bsz, 1, 3 * D_MODEL) for l in range(DEPTH)]

    row = lambda a: a.reshape(a.shape[0], 1, a.shape[-1])
    W = {
        "norm_mix": row(norm_mix), "norm_mlp": row(norm_mlp),
        "mlp_w1": mlp_w1.reshape(DEPTH * D_MODEL, D_FF), "mlp_w2": mlp_w2.reshape(DEPTH * D_FF, D_MODEL),
        "w_in_even": w_in_even, "w_in_odd": w_in_odd,
        "w_out_even": w_out_even.reshape(-1, D_MODEL), "w_out_odd": w_out_odd.reshape(-1, D_MODEL),
        "sconv_w": sconv_w,
        "q_norm": jnp.tile(q_norm, (1, 2))[:, None, :], "k_norm": jnp.tile(k_norm, (1, 2))[:, None, :],
        "lambda_q1": lambda_q1, "lambda_k1": lambda_k1, "lambda_q2": lambda_q2, "lambda_k2": lambda_k2,
        "subln": subln,
        "lru": [(lru_conv_w[e], lru_conv_b[e][None, :], _block_diag(lru_wa[e]), lru_ba[e].reshape(1, W_LRU),
                 _block_diag(lru_wx[e]), lru_bx[e].reshape(1, W_LRU), lru_lam[e][None, :])
                for e in range(N_EVEN)],
    }

    cast = {}
    done = _run_trunks({
        "sample": _trunk(sample, _to_time_major(x_sample), mods_s, past_len + jnp.arange(dec_seq),
                         (state_lru_h, state_lru_conv, state_sconv), (cache_k, cache_v, page_table), W, cast),
        "prompt": _trunk(prompt, x_prompt.reshape(prompt.rows, D_MODEL), mods_p, jnp.arange(seq),
                         (None, None, None), None, W, cast),
    })
    (y_s, o_s), (y_p, o_p) = done["sample"], done["prompt"]
    return (y_p, y_s, o_p["k"], o_p["v"], o_s["k"], o_s["v"], o_p["h"], o_s["h"],
            o_p["cl"], o_s["cl"], o_p["sc"], o_s["sc"])
```

```python
import functools
import math

import jax
import jax.numpy as jnp
from jax import lax
from jax.experimental import pallas as pl
from jax.experimental.pallas import tpu as pltpu

D_MODEL = 2048
DEPTH = 2
PAGE_SIZE = 128
N_EVEN = (DEPTH + 1) // 2
N_ODD = DEPTH // 2
W_LRU = D_MODEL // 2
H_LRU = 16
BLK_LRU = W_LRU // H_LRU
CONV_LRU = 4
LRU_C = 8.0
H_ATT = 8
DV = (D_MODEL // 2) // H_ATT
DH = DV // 2
W_ATT = H_ATT * DV
W_SC = D_MODEL
SC_CONV = 3
D_FF = 4 * D_MODEL
ROPE_THETA = 10000.0
RMS_EPS = 1e-6
NEG_INF = -1e30

F32 = jnp.float32
BF16 = jnp.bfloat16

SUBLANES = 8
LANES = 128
MIB = 1 << 20
VMEM_LIMIT = 56 * MIB

TN_ADA = 2048
TM_PROJ = 1024
TM_OUT = 512
TF_MLP = 1024
TN_ODD = 512
TN_ODD_CAST = 256
QK_COLS = 256
TQ_ATT = 512
HEADS_PER_STEP = 2
T_LRU = 512
GATE_CHUNK = 256
EPILOGUE_ROWS = 16


def _params(sem):
    return pltpu.CompilerParams(dimension_semantics=sem, vmem_limit_bytes=VMEM_LIMIT)


def _dot(a, b):
    return jnp.dot(a, b, preferred_element_type=F32)


def _dot_nt(a, b):
    return lax.dot_general(a, b, (((1,), (1,)), ((), ())), preferred_element_type=F32)


def _modnorm(x, g, shift, scale):
    ms = jnp.mean(x * x, axis=-1, keepdims=True)
    y = x * lax.rsqrt(ms + RMS_EPS)
    return (y * g) * (1.0 + scale) + shift


def _mod_slabs(n_rows, mod_rows):
    out = []
    for r in range(0, n_rows, EPILOGUE_ROWS):
        m = slice(None) if mod_rows == 1 else slice(r % mod_rows, r % mod_rows + EPILOGUE_ROWS)
        out.append((slice(r, r + EPILOGUE_ROWS), m))
    return out


def _modnorm_rows(x_of, g, shift_ref, scale_ref, mrows):
    parts = []
    for lo in range(0, EPILOGUE_ROWS, SUBLANES):
        if isinstance(mrows, slice) and mrows.start is not None:
            m = slice(mrows.start + lo, mrows.start + lo + SUBLANES)
        else:
            m = mrows
        parts.append(_modnorm(x_of(lo, lo + SUBLANES), g, shift_ref[m, :], scale_ref[m, :]))
    return jnp.concatenate(parts, axis=0).astype(BF16)


def _ada_kernel(c_ref, w_ref, b_ref, o_ref):
    c = c_ref[...]
    a = (c * jax.nn.sigmoid(c)).astype(BF16)
    o_ref[...] = _dot(a, w_ref[...].astype(BF16)) + b_ref[...]


def _ada(c_all, w, b):
    rows = c_all.shape[0]
    tn = TN_ADA
    return pl.pallas_call(
        _ada_kernel,
        grid=(DEPTH, 3 * D_MODEL // tn),
        in_specs=[
            pl.BlockSpec((rows, D_MODEL), lambda l, j: (0, 0)),
            pl.BlockSpec((None, D_MODEL, tn), lambda l, j: (l, 0, j)),
            pl.BlockSpec((None, 1, tn), lambda l, j: (l, 0, j)),
        ],
        out_specs=pl.BlockSpec((None, rows, tn), lambda l, j: (l, 0, j)),
        out_shape=jax.ShapeDtypeStruct((DEPTH, rows, 3 * D_MODEL), F32),
        compiler_params=_params(("arbitrary", "arbitrary")),
        name="ada_modulation",
    )(c_all, w, b.reshape(DEPTH, 1, 3 * D_MODEL))


class _Group:
    def __init__(self, batch, seq, time_major):
        self.batch = batch
        self.seq = seq
        self.rows = batch * seq
        self.time_major = time_major

    def mod_spec(self, tm, part, row_axis=0):
        if self.time_major:
            assert tm % self.batch == 0
            return pl.BlockSpec((self.batch, D_MODEL), lambda *g: (0, part))
        tiles_per_batch = self.seq // tm
        return pl.BlockSpec((None, 1, D_MODEL), lambda *g: (g[row_axis] // tiles_per_batch, 0, part))


def _modnorm_kernel(x_ref, g_ref, shift_ref, scale_ref, h_ref):
    g = g_ref[...]
    for rows, mrows in _mod_slabs(x_ref.shape[0], shift_ref.shape[0]):
        x_of = lambda lo, hi, r0=rows.start: x_ref[r0 + lo:r0 + hi, :]
        h_ref[rows, :] = _modnorm_rows(x_of, g, shift_ref, scale_ref, mrows)


def _first_norm(grp, x, g, mod):
    tm = TM_OUT
    return pl.pallas_call(
        _modnorm_kernel,
        grid=(grp.rows // tm,),
        in_specs=[
            pl.BlockSpec((tm, D_MODEL), lambda i: (i, 0)),
            pl.BlockSpec((1, D_MODEL), lambda i: (0, 0)),
            grp.mod_spec(tm, 0),
            grp.mod_spec(tm, 1),
        ],
        out_specs=pl.BlockSpec((tm, D_MODEL), lambda i: (i, 0)),
        out_shape=jax.ShapeDtypeStruct((grp.rows, D_MODEL), BF16),
        compiler_params=_params(("arbitrary",)),
        name="first_norm",
    )(x, g, mod, mod)


def _weight_cols(w_ref, wb_ref, cols):
    w = w_ref[:, cols]
    if wb_ref is None:
        return w
    w = w.astype(BF16)
    wb_ref[:, cols] = w
    return w


def _qk_heads(h_ref, w_ref, wb_ref, g_ref, cos_ref, sin_ref):
    h = h_ref[...]
    g, cos, sin = g_ref[...], cos_ref[...], sin_ref[...]
    tm = h.shape[0]
    lane = lax.broadcasted_iota(jnp.int32, (tm, LANES), 1)
    first_map = lane < DH
    first_half = (lane % DH) < (DH // 2)
    out = []
    for c in range(W_ATT // QK_COLS):
        acc = _dot(h, _weight_cols(w_ref, wb_ref, slice(c * QK_COLS, (c + 1) * QK_COLS)))
        for hd in range(QK_COLS // LANES):
            xs = acc[:, hd * LANES:(hd + 1) * LANES]
            sq = xs * xs
            lo = jnp.sum(jnp.where(first_map, sq, 0.0), axis=-1, keepdims=True)
            hi = jnp.sum(jnp.where(first_map, 0.0, sq), axis=-1, keepdims=True)
            ms = jnp.where(first_map, lo, hi) * (1.0 / DH)
            y = xs * lax.rsqrt(ms + RMS_EPS) * g
            rot = jnp.where(first_half,
                            pltpu.roll(y, LANES - DH // 2, axis=1),
                            pltpu.roll(y, DH // 2, axis=1))
            out.append(y * cos + rot * sin)
    return out, first_map


def _q_prompt_kernel(h_ref, w_ref, g_ref, cos_ref, sin_ref, q1_ref, q2_ref):
    slabs, first_map = _qk_heads(h_ref, w_ref, None, g_ref, cos_ref, sin_ref)
    for hd, y in enumerate(slabs):
        y = y * (DH ** -0.5)
        cols = slice(hd * LANES, (hd + 1) * LANES)
        q1_ref[:, cols] = jnp.where(first_map, y, 0.0).astype(BF16)
        q2_ref[:, cols] = jnp.where(first_map, 0.0, y).astype(BF16)


def _k_prompt_kernel(h_ref, w_ref, g_ref, cos_ref, sin_ref, kf_ref, kb_ref):
    slabs, _ = _qk_heads(h_ref, w_ref, None, g_ref, cos_ref, sin_ref)
    for hd, y in enumerate(slabs):
        cols = slice(hd * LANES, (hd + 1) * LANES)
        kf_ref[:, cols] = y
        kb_ref[:, cols] = y.astype(BF16)


def _qk_sample_kernel(h_ref, w_ref, g_ref, cos_ref, sin_ref, o_ref, wb_ref, *, scale):
    slabs, _ = _qk_heads(h_ref, w_ref, wb_ref, g_ref, cos_ref, sin_ref)
    for hd, y in enumerate(slabs):
        o_ref[:, hd * LANES:(hd + 1) * LANES] = y * scale


def _v_prompt_kernel(h_ref, w_ref, vf_ref, vb_ref):
    acc = _dot(h_ref[...], w_ref[...])
    vf_ref[...] = acc
    vb_ref[...] = acc.astype(BF16)


def _plain_sample_kernel(h_ref, w_ref, o_ref, wb_ref):
    o_ref[...] = _dot(h_ref[...], _weight_cols(w_ref, wb_ref, slice(None)))


def _proj_call(kernel, grp, h, w, n_col_blocks, extra, extra_specs, out_dtypes, name):
    tm = min(TM_PROJ, grp.rows)
    tn = W_ATT
    out_spec = pl.BlockSpec((tm, tn), lambda i, j: (i, j))
    out_specs = [out_spec] * len(out_dtypes)
    out_shape = [jax.ShapeDtypeStruct((grp.rows, tn * n_col_blocks), dt) for dt in out_dtypes]
    w_cols = pl.BlockSpec((D_MODEL, tn), lambda i, j: (0, j))
    if isinstance(w, tuple):
        w, idx, col_block = w
        w_spec = pl.BlockSpec((None, D_MODEL, tn), lambda i, j: (idx, 0, col_block + j))
        out_specs.append(w_cols)
        out_shape.append(jax.ShapeDtypeStruct((D_MODEL, tn * n_col_blocks), BF16))
    else:
        w_spec = w_cols
    return pl.pallas_call(
        kernel,
        grid=(grp.rows // tm, n_col_blocks),
        in_specs=[pl.BlockSpec((tm, D_MODEL), lambda i, j: (i, 0)), w_spec] + extra_specs,
        out_specs=out_specs,
        out_shape=out_shape,
        compiler_params=_params(("arbitrary", "arbitrary")),
        name=name,
    )(h, w, *extra)


def _rope_specs(grp, cos):
    tm = min(TM_PROJ, grp.rows)
    tiles = cos.shape[0] // tm
    rope = pl.BlockSpec((tm, LANES), lambda i, j: (i % tiles, 0))
    return [pl.BlockSpec((1, LANES), lambda i, j: (0, 0)), rope, rope]


def _even_in_proj_sample(grp, h, w, e, qg, kg, cos, sin):
    specs = _rope_specs(grp, cos)
    q_kernel = functools.partial(_qk_sample_kernel, scale=DH ** -0.5)
    k_kernel = functools.partial(_qk_sample_kernel, scale=1.0)
    q, wq = _proj_call(q_kernel, grp, h, (w, e, 0), 1, (qg, cos, sin), specs, (F32,), "q_proj")
    kf, wk = _proj_call(k_kernel, grp, h, (w, e, 1), 1, (kg, cos, sin), specs, (F32,), "k_proj")
    vf, wv = _proj_call(_plain_sample_kernel, grp, h, (w, e, 2), 1, (), [], (F32,), "v_proj")
    xg, wl = _proj_call(_plain_sample_kernel, grp, h, (w, e, 3), 2, (), [], (F32,), "lru_proj")
    return q, kf, vf, xg, (wq, wk, wv, wl)


def _even_in_proj_prompt(grp, h, wq, wk, wv, qg, kg, cos, sin):
    specs = _rope_specs(grp, cos)
    q1, q2 = _proj_call(_q_prompt_kernel, grp, h, wq, 1, (qg, cos, sin), specs, (BF16, BF16), "q_proj")
    kf, kb = _proj_call(_k_prompt_kernel, grp, h, wk, 1, (kg, cos, sin), specs, (F32, BF16), "k_proj")
    vf, vb = _proj_call(_v_prompt_kernel, grp, h, wv, 1, (), [], (F32, BF16), "v_proj")
    return q1, q2, kf, kb, vf, vb


def _diff_lambda(lq1_ref, lk1_ref, lq2_ref, lk2_ref, lam_init):
    a = jnp.sum(lq1_ref[...] * lk1_ref[...], axis=-1, keepdims=True)
    b = jnp.sum(lq2_ref[...] * lk2_ref[...], axis=-1, keepdims=True)
    return jnp.exp(a) - jnp.exp(b) + lam_init


def _sub_norm(o, g, lam_init):
    ms = jnp.mean(o * o, axis=-1, keepdims=True)
    return (o * lax.rsqrt(ms + RMS_EPS) * g) * (1.0 - lam_init)


def _attn_prompt_kernel(lq1_ref, lk1_ref, lq2_ref, lk2_ref, g_ref, q1_ref, q2_ref, k_ref, v_ref, *cast_refs,
                        tq, n_tiles, lam_init):
    n_cast = len(cast_refs) // 2
    o_ref = cast_refs[n_cast]
    qi = pl.program_id(2)
    lam = _diff_lambda(lq1_ref, lk1_ref, lq2_ref, lk2_ref, lam_init)
    row = lax.broadcasted_iota(jnp.int32, (tq, tq), 0)
    col = lax.broadcasted_iota(jnp.int32, (tq, tq), 1)
    causal = col <= row

    def body(c):
        for src_ref, dst_ref in zip(cast_refs[:n_cast], cast_refs[n_cast + 1:]):
            if len(dst_ref.shape) == 2:
                dst_ref[...] = src_ref[...].astype(BF16)
            else:
                tile = dst_ref.shape[2]
                for j in range(dst_ref.shape[0]):
                    dst_ref[j] = src_ref[:, j * tile:(j + 1) * tile].astype(BF16)
        n_past = c * tq
        for hd in range(o_ref.shape[1] // DV):
            cols = slice(hd * DV, (hd + 1) * DV)
            kd = k_ref[n_past:n_past + tq, cols]
            vd = v_ref[n_past:n_past + tq, cols]
            maps = []
            for q_ref in (q1_ref, q2_ref):
                q = q_ref[:, cols]
                sd = jnp.where(causal, _dot_nt(q, kd), NEG_INF)
                m = jnp.max(sd, axis=-1, keepdims=True)
                if c > 0:
                    sp = _dot_nt(q, k_ref[0:n_past, cols])
                    m = jnp.maximum(m, jnp.max(sp, axis=-1, keepdims=True))
                pd = jnp.exp(sd - m)
                l = jnp.sum(pd, axis=-1, keepdims=True)
                acc = _dot(pd.astype(BF16), vd)
                if c > 0:
                    pp = jnp.exp(sp - m)
                    l = l + jnp.sum(pp, axis=-1, keepdims=True)
                    acc = acc + _dot(pp.astype(BF16), v_ref[0:n_past, cols])
                maps.append(acc / l)
            o = maps[0] - lam * maps[1]
            o_ref[:, cols] = _sub_norm(o, g_ref[...], lam_init).astype(BF16)

    for c in range(n_tiles):
        pl.when(qi == c)(functools.partial(body, c))


def _attn_prompt(grp, q1, q2, kb, vb, lam_vecs, subln, lam_init, to_cast):
    tq = TQ_ATT
    tiles = grp.seq // tq
    n_groups = H_ATT // HEADS_PER_STEP
    width = HEADS_PER_STEP * DV
    steps = grp.batch * n_groups * tiles
    vec64 = pl.BlockSpec((1, DH), lambda b, h, i: (0, 0))
    q_spec = pl.BlockSpec((tq, width), lambda b, h, i: (b * tiles + i, h))
    kv_spec = pl.BlockSpec((grp.seq, width), lambda b, h, i: (b, h))
    step = lambda b, h, i: (b * n_groups + h) * tiles + i
    src_specs, dst_specs, dst_shapes = [], [], []
    for a, tile in to_cast:
        rows, cols = a.shape
        assert rows % (steps * 2 * SUBLANES) == 0
        src_specs.append(pl.BlockSpec((rows // steps, cols), lambda b, h, i: (step(b, h, i), 0)))
        if tile is None:
            dst_specs.append(src_specs[-1])
            dst_shapes.append(jax.ShapeDtypeStruct((rows, cols), BF16))
        else:
            dst_specs.append(pl.BlockSpec((cols // tile, rows // steps, tile), lambda b, h, i: (0, step(b, h, i), 0)))
            dst_shapes.append(jax.ShapeDtypeStruct((cols // tile, rows, tile), BF16))
    res = pl.pallas_call(
        functools.partial(_attn_prompt_kernel, tq=tq, n_tiles=tiles, lam_init=lam_init),
        grid=(grp.batch, n_groups, tiles),
        in_specs=[vec64] * 4 + [pl.BlockSpec((1, DV), lambda b, h, i: (0, 0)), q_spec, q_spec, kv_spec, kv_spec]
        + src_specs,
        out_specs=[q_spec] + dst_specs,
        out_shape=[jax.ShapeDtypeStruct((grp.rows, W_ATT), BF16)] + dst_shapes,
        compiler_params=_params(("arbitrary", "arbitrary", "arbitrary")),
        name="attn_prompt",
    )(*lam_vecs, subln, q1, q2, kb, vb, *[a for a, _ in to_cast])
    return res[0], res[1:]


def _attn_sample_kernel(pt_ref, lq1_ref, lk1_ref, lq2_ref, lk2_ref, g_ref, q_ref, kn_ref, vn_ref, *rest,
                        n_pages, lam_init):
    k_refs = rest[:n_pages]
    v_refs = rest[n_pages:2 * n_pages]
    o_ref = rest[2 * n_pages]
    del pt_ref
    n_q = q_ref.shape[0]
    half = H_ATT // 2
    per_head = 2 * n_q
    n_past = n_pages * PAGE_SIZE
    lam = _diff_lambda(lq1_ref, lk1_ref, lq2_ref, lk2_ref, lam_init)
    g = g_ref[...]
    lane = lax.broadcasted_iota(jnp.int32, (n_q, LANES), 1)
    row = lax.broadcasted_iota(jnp.int32, (2 * per_head, 1), 0)
    t_of_row = row % n_q
    second = row >= per_head
    col = lax.broadcasted_iota(jnp.int32, (2 * per_head, 2 * n_past), 1)
    own_head = (col % 2) == (lax.broadcasted_iota(jnp.int32, (2 * per_head, 2 * n_past), 0) // per_head)
    for hp in range(half):
        heads = (hp, hp + half)
        qbd = []
        for hd in heads:
            q = q_ref[:, hd, :]
            qbd += [jnp.where(lane < DH, q, 0.0), jnp.where(lane < DH, 0.0, q)]
        qbd = jnp.concatenate(qbd, axis=0)
        pair_rows = pl.ds(hp, 2 * PAGE_SIZE, stride=half)
        kp = jnp.concatenate([r[pair_rows, :].astype(BF16) for r in k_refs], axis=0)
        vp = jnp.concatenate([r[pair_rows, :].astype(BF16) for r in v_refs], axis=0)
        s = jnp.where(own_head, _dot_nt(qbd.astype(BF16), kp), NEG_INF)
        kn = [kn_ref[:, hd, :] for hd in heads]
        vn = [vn_ref[:, hd, :] for hd in heads]
        s_new = []
        for j in range(n_q):
            kj = jnp.where(second, kn[1][j:j + 1, :], kn[0][j:j + 1, :])
            s_new.append(jnp.where(t_of_row >= j, jnp.sum(qbd * kj, axis=-1, keepdims=True), NEG_INF))
        m = jnp.max(s, axis=-1, keepdims=True)
        for sj in s_new:
            m = jnp.maximum(m, sj)
        p = jnp.exp(s - m)
        l = jnp.sum(p, axis=-1, keepdims=True)
        acc = _dot(p.astype(BF16), vp)
        for j, sj in enumerate(s_new):
            pj = jnp.exp(sj - m)
            l = l + pj
            acc = acc + pj * jnp.where(second, vn[1][j:j + 1, :], vn[0][j:j + 1, :])
        o = acc / l
        for a, hd in enumerate(heads):
            oa = o[a * per_head:a * per_head + n_q] - lam * o[a * per_head + n_q:(a + 1) * per_head]
            o_ref[:, hd, :] = _sub_norm(oa, g, lam_init)


def _attn_sample(grp, e, q, kf, vf, cache_k, cache_v, page_table, lam_vecs, subln, lam_init):
    n_pages = page_table.shape[1]
    tok = (grp.seq, grp.batch, H_ATT, LANES)
    vec64 = pl.BlockSpec((1, DH), lambda b, pt: (0, 0))
    new_spec = pl.BlockSpec((grp.seq, None, H_ATT, LANES), lambda b, pt: (0, b, 0, 0))

    def page_spec(p):
        return pl.BlockSpec((None, None, PAGE_SIZE * H_ATT, LANES),
                            lambda b, pt: (e, pt[b * n_pages + p], 0, 0))

    pages = [page_spec(p) for p in range(n_pages)]
    page_rows = cache_k.shape[:2] + (PAGE_SIZE * H_ATT, LANES)
    cache_k = cache_k.reshape(page_rows)
    cache_v = cache_v.reshape(page_rows)
    out = pl.pallas_call(
        functools.partial(_attn_sample_kernel, n_pages=n_pages, lam_init=lam_init),
        grid_spec=pltpu.PrefetchScalarGridSpec(
            num_scalar_prefetch=1,
            grid=(grp.batch,),
            in_specs=[vec64] * 4 + [pl.BlockSpec((1, DV), lambda b, pt: (0, 0)), new_spec, new_spec, new_spec]
            + pages + pages,
            out_specs=new_spec,
        ),
        out_shape=jax.ShapeDtypeStruct(tok, F32),
        compiler_params=_params(("arbitrary",)),
        name="attn_sample",
    )(page_table.reshape(-1), *lam_vecs, subln, q.reshape(tok), kf.reshape(tok), vf.reshape(tok),
      *([cache_k] * n_pages), *([cache_v] * n_pages))
    return out.reshape(grp.rows, W_ATT)


def _shifted(u, hist, s):
    row = lax.broadcasted_iota(jnp.int32, u.shape, 0)
    return jnp.where(row < s, pltpu.roll(hist, s, axis=0), pltpu.roll(u, s, axis=0))


def _scan_rows(a, b, carry):
    r, w = a.shape
    groups = r // SUBLANES
    a = a.reshape(groups, SUBLANES, w)
    b = b.reshape(groups, SUBLANES, w)
    sub = lax.broadcasted_iota(jnp.int32, a.shape, 1)
    s = 1
    while s < SUBLANES:
        valid = sub >= s
        b = jnp.where(valid, a * pltpu.roll(b, s, axis=1) + b, b)
        a = jnp.where(valid, a * pltpu.roll(a, s, axis=1), a)
        s *= 2
    hs = []
    for grp_i in range(groups):
        hg = a[grp_i] * carry + b[grp_i]
        hs.append(hg)
        carry = hg[SUBLANES - 1:, :]
    return hs


def _lru_gates(xc, wa, ba, wx, bx, lam):
    xb = xc.astype(BF16)
    r_parts, i_parts = [], []
    for c, (wa_c, wx_c) in enumerate(zip(wa, wx)):
        cols = slice(c * GATE_CHUNK, (c + 1) * GATE_CHUNK)
        r_parts.append(_dot(xb[:, cols], wa_c))
        i_parts.append(_dot(xb[:, cols], wx_c))
    cat = (lambda parts: parts[0] if len(parts) == 1 else jnp.concatenate(parts, axis=1))
    r = jax.nn.sigmoid(cat(r_parts) + ba)
    i = jax.nn.sigmoid(cat(i_parts) + bx)
    log_a = (-LRU_C) * r * jax.nn.softplus(-lam)
    a = jnp.exp(log_a)
    drive = jnp.sqrt(1.0 - a * a) * (i * xc)
    return a, drive


def _lru_prompt_kernel(h_ref, wl_ref, cw_ref, cb_ref, wa_ref, ba_ref, wx_ref, bx_ref, lam_ref,
                       y_ref, hlast_ref, ctail_ref, hist_ref, hprev_ref):
    t = pl.program_id(1)
    rows = h_ref.shape[0]

    @pl.when(t == 0)
    def _():
        hist_ref[...] = jnp.zeros_like(hist_ref)
        hprev_ref[...] = jnp.zeros_like(hprev_ref)

    hin = h_ref[...]
    xl = _dot(hin, wl_ref[:, :W_LRU])
    gl = _dot(hin, wl_ref[:, W_LRU:])
    hist = hist_ref[...]
    xc = xl * cw_ref[CONV_LRU - 1:CONV_LRU, :] + cb_ref[...]
    for s in range(1, CONV_LRU):
        xc = xc + _shifted(xl, hist, s) * cw_ref[CONV_LRU - 1 - s:CONV_LRU - s, :]
    n_chunks = W_LRU // GATE_CHUNK
    a, drive = _lru_gates(xc, [wa_ref[c] for c in range(n_chunks)], ba_ref[...],
                          [wx_ref[c] for c in range(n_chunks)], bx_ref[...], lam_ref[...])
    hs = _scan_rows(a, drive, hprev_ref[SUBLANES - 1:, :])
    y_ref[...] = (jax.nn.gelu(gl) * jnp.concatenate(hs, axis=0)).astype(BF16)
    hist_ref[...] = xl
    hprev_ref[...] = hs[-1]
    hlast_ref[...] = hs[-1]
    ctail_ref[...] = xl[rows - SUBLANES:, :]


def _lru_prompt(grp, h, wl, lru_w):
    tt = T_LRU
    tiles = grp.seq // tt
    n_chunks = W_LRU // GATE_CHUNK

    def full(shape):
        return pl.BlockSpec(shape, lambda b, t: (0,) * len(shape))

    tail = pl.BlockSpec((None, SUBLANES, W_LRU), lambda b, t: (b, 0, 0))
    tail_shape = jax.ShapeDtypeStruct((grp.batch, SUBLANES, W_LRU), F32)
    y, hlast, ctail = pl.pallas_call(
        _lru_prompt_kernel,
        grid=(grp.batch, tiles),
        in_specs=[pl.BlockSpec((tt, D_MODEL), lambda b, t: (b * tiles + t, 0)),
                  full((D_MODEL, 2 * W_LRU)),
                  full((CONV_LRU, W_LRU)), full((1, W_LRU)), full((n_chunks, GATE_CHUNK, GATE_CHUNK)),
                  full((1, W_LRU)), full((n_chunks, GATE_CHUNK, GATE_CHUNK)), full((1, W_LRU)), full((1, W_LRU))],
        out_specs=[pl.BlockSpec((tt, W_LRU), lambda b, t: (b * tiles + t, 0)), tail, tail],
        out_shape=[jax.ShapeDtypeStruct((grp.rows, W_LRU), BF16), tail_shape, tail_shape],
        scratch_shapes=[pltpu.VMEM((tt, W_LRU), F32), pltpu.VMEM((SUBLANES, W_LRU), F32)],
        compiler_params=_params(("arbitrary", "arbitrary")),
        name="lru_prompt",
    )(h, wl, *lru_w)
    return y, hlast[:, SUBLANES - 1, :], ctail[:, SUBLANES - (CONV_LRU - 1):, :]


def _lru_sample_kernel(xl_ref, gl_ref, hist_ref, h0_ref, cw_ref, cb_ref, wa_ref, ba_ref, wx_ref, bx_ref, lam_ref,
                       y_ref, h_ref, *, n_t):
    nb = h0_ref.shape[0]

    def slab(t):
        return slice(t * nb, (t + 1) * nb)

    def u(t):
        return xl_ref[slab(t), :] if t >= 0 else hist_ref[slab(CONV_LRU - 1 + t), :]

    xc = []
    for t in range(n_t):
        acc = u(t) * cw_ref[CONV_LRU - 1:CONV_LRU, :] + cb_ref[...]
        for s in range(1, CONV_LRU):
            acc = acc + u(t - s) * cw_ref[CONV_LRU - 1 - s:CONV_LRU - s, :]
        xc.append(acc)
    xc = jnp.concatenate(xc, axis=0)
    a, drive = _lru_gates(xc, [wa_ref[0]], ba_ref[...], [wx_ref[0]], bx_ref[...], lam_ref[...])
    h = h0_ref[...]
    for t in range(n_t):
        h = a[slab(t), :] * h + drive[slab(t), :]
        y_ref[slab(t), :] = (jax.nn.gelu(gl_ref[slab(t), :]) * h).astype(BF16)
    h_ref[...] = h


def _lru_sample(grp, xg, hist, h0, lru_w):
    gc = GATE_CHUNK
    n_chunks = W_LRU // gc
    vec = pl.BlockSpec((1, gc), lambda c: (0, c))
    gate_w = pl.BlockSpec((1, gc, gc), lambda c: (c, 0, 0))
    return pl.pallas_call(
        functools.partial(_lru_sample_kernel, n_t=grp.seq),
        grid=(n_chunks,),
        in_specs=[pl.BlockSpec((grp.rows, gc), lambda c: (0, c)),
                  pl.BlockSpec((grp.rows, gc), lambda c: (0, n_chunks + c)),
                  pl.BlockSpec((hist.shape[0], gc), lambda c: (0, c)),
                  pl.BlockSpec((grp.batch, gc), lambda c: (0, c)),
                  pl.BlockSpec((CONV_LRU, gc), lambda c: (0, c)), vec, gate_w, vec, gate_w, vec, vec],
        out_specs=[pl.BlockSpec((grp.rows, gc), lambda c: (0, c)), pl.BlockSpec((grp.batch, gc), lambda c: (0, c))],
        out_shape=[jax.ShapeDtypeStruct((grp.rows, W_LRU), BF16), jax.ShapeDtypeStruct((grp.batch, W_LRU), F32)],
        compiler_params=_params(("arbitrary",)),
        name="lru_sample",
    )(xg, xg, hist, h0, *lru_w)


def _residual_epilogue(acc, x_ref, gate_ref, norm, x_out_ref, h_out_ref):
    g = None if norm is None else norm[0][...]
    for rows, mrows in _mod_slabs(x_ref.shape[0], gate_ref.shape[0]):
        r0 = rows.start

        def x_new_of(lo, hi, r0=r0, mrows=mrows):
            m = mrows if mrows.start is None else slice(mrows.start + lo, mrows.start + hi)
            x_new = x_ref[r0 + lo:r0 + hi, :] + gate_ref[m, :] * acc[r0 + lo:r0 + hi, :]
            x_out_ref[r0 + lo:r0 + hi, :] = x_new
            return x_new

        if h_out_ref is not None:
            h_out_ref[rows, :] = _modnorm_rows(x_new_of, g, norm[1], norm[2], mrows)
        else:
            for lo in range(0, EPILOGUE_ROWS, SUBLANES):
                x_new_of(lo, lo + SUBLANES)


def _out_proj_kernel(*refs, n_in):
    a_refs = refs[:n_in]
    w_ref, x_ref, gate_ref, g_ref, shift_ref, scale_ref, x_out_ref, h_out_ref = refs[n_in:]
    acc = None
    k0 = 0
    for a_ref in a_refs:
        k = a_ref.shape[1]
        part = _dot(a_ref[...].astype(BF16), w_ref[k0:k0 + k, :])
        acc = part if acc is None else acc + part
        k0 += k
    _residual_epilogue(acc, x_ref, gate_ref, (g_ref, shift_ref, scale_ref), x_out_ref, h_out_ref)


def _out_proj(grp, acts, w, x, gate_mod, g_next, next_mod, name):
    w, idx = w
    tm = TM_OUT
    row = pl.BlockSpec((tm, D_MODEL), lambda i: (i, 0))
    return pl.pallas_call(
        functools.partial(_out_proj_kernel, n_in=len(acts)),
        grid=(grp.rows // tm,),
        in_specs=[pl.BlockSpec((tm, a.shape[1]), lambda i: (i, 0)) for a in acts] + [
            pl.BlockSpec((D_MODEL, D_MODEL), lambda i: (idx, 0)),
            row,
            grp.mod_spec(tm, 2),
            pl.BlockSpec((1, D_MODEL), lambda i: (0, 0)),
            grp.mod_spec(tm, 0),
            grp.mod_spec(tm, 1),
        ],
        out_specs=[row, row],
        out_shape=[jax.ShapeDtypeStruct((grp.rows, D_MODEL), F32), jax.ShapeDtypeStruct((grp.rows, D_MODEL), BF16)],
        compiler_params=_params(("arbitrary",)),
        name=name,
    )(*acts, w, x, gate_mod, g_next, next_mod, next_mod)


def _mlp_kernel(*refs, with_norm):
    if with_norm:
        h_ref, w1_ref, w2_ref, x_ref, gate_ref, g_ref, shift_ref, scale_ref, x_out_ref, h_out_ref = refs
        norm = (g_ref, shift_ref, scale_ref)
    else:
        h_ref, w1_ref, w2_ref, x_ref, gate_ref, x_out_ref = refs
        norm, h_out_ref = None, None
    f = pl.program_id(1)

    @pl.when(f == 0)
    def _():
        x_out_ref[...] = jnp.zeros_like(x_out_ref)

    hid = jnp.maximum(_dot(h_ref[...], w1_ref[...]), 0.0)
    x_out_ref[...] += _dot((hid * hid).astype(BF16), w2_ref[...])

    @pl.when(f == pl.num_programs(1) - 1)
    def _():
        _residual_epilogue(x_out_ref, x_ref, gate_ref, norm, x_out_ref, h_out_ref)


def _mlp(grp, h, w1, w2, layer, x, gate_mod, g_next, next_mod, name):
    tm, tf = TM_OUT, TF_MLP
    with_norm = g_next is not None
    row = pl.BlockSpec((tm, D_MODEL), lambda i, f: (i, 0))
    n_f = D_FF // tf
    in_specs = [row,
                pl.BlockSpec((None, D_MODEL, tf), lambda i, f: (f, layer, 0)),
                pl.BlockSpec((tf, D_MODEL), lambda i, f: (layer * n_f + f, 0)),
                row,
                grp.mod_spec(tm, 2)]
    args = [h, w1, w2, x, gate_mod]
    out_specs = [row]
    out_shape = [jax.ShapeDtypeStruct((grp.rows, D_MODEL), F32)]
    if with_norm:
        in_specs += [pl.BlockSpec((1, D_MODEL), lambda i, f: (0, 0)), grp.mod_spec(tm, 0), grp.mod_spec(tm, 1)]
        args += [g_next, next_mod, next_mod]
        out_specs.append(row)
        out_shape.append(jax.ShapeDtypeStruct((grp.rows, D_MODEL), BF16))
    return pl.pallas_call(
        functools.partial(_mlp_kernel, with_norm=with_norm),
        grid=(grp.rows // tm, D_FF // tf),
        in_specs=in_specs,
        out_specs=out_specs,
        out_shape=out_shape,
        compiler_params=_params(("arbitrary", "arbitrary")),
        name=name,
    )(*args)


def _sconv_prompt_kernel(h_ref, wb_ref, wc_ref, wx_ref, cw_ref, g_ref, tail_ref, hist_ref, *, tiles_per_batch):
    i = pl.program_id(1)
    rows = h_ref.shape[0]

    @pl.when(i % tiles_per_batch == 0)
    def _():
        hist_ref[...] = jnp.zeros_like(hist_ref)

    h = h_ref[...]
    u = _dot(h, wc_ref[...]) * _dot(h, wx_ref[...])
    hist = hist_ref[...]
    conv = u * cw_ref[SC_CONV - 1:SC_CONV, :]
    for s in range(1, SC_CONV):
        conv = conv + _shifted(u, hist, s) * cw_ref[SC_CONV - 1 - s:SC_CONV - s, :]
    g_ref[...] = (_dot(h, wb_ref[...]) * conv).astype(BF16)
    hist_ref[...] = u
    tail_ref[...] = u[rows - SUBLANES:, :]


def _sconv_sample_kernel(h_ref, wb_ref, wc_ref, wx_ref, cw_ref, hist_ref, g_ref, tail_ref,
                         wb_out_ref, wc_out_ref, wx_out_ref, *, n_t):
    h = h_ref[...]
    every = slice(None)
    u = _dot(h, _weight_cols(wc_ref, wc_out_ref, every)) * _dot(h, _weight_cols(wx_ref, wx_out_ref, every))
    b = _dot(h, _weight_cols(wb_ref, wb_out_ref, every))
    nb = h.shape[0] // n_t

    def slab(t):
        return slice(t * nb, (t + 1) * nb)

    def ut(t):
        return u[slab(t), :] if t >= 0 else hist_ref[slab(SC_CONV - 1 + t), :]

    for t in range(n_t):
        conv = ut(t) * cw_ref[SC_CONV - 1:SC_CONV, :]
        for s in range(1, SC_CONV):
            conv = conv + ut(t - s) * cw_ref[SC_CONV - 1 - s:SC_CONV - s, :]
        g_ref[slab(t), :] = (b[slab(t), :] * conv).astype(BF16)
    tail_ref[...] = u[(n_t - (SC_CONV - 1)) * nb:, :]


def _sconv_prompt(grp, h, w_parts, conv_w):
    tm, tn = TM_PROJ, TN_ODD
    tiles_per_batch = grp.seq // tm
    w_cols = pl.BlockSpec((D_MODEL, tn), lambda n, i: (0, n))
    g, tail = pl.pallas_call(
        functools.partial(_sconv_prompt_kernel, tiles_per_batch=tiles_per_batch),
        grid=(W_SC // tn, grp.rows // tm),
        in_specs=[pl.BlockSpec((tm, D_MODEL), lambda n, i: (i, 0)), w_cols, w_cols, w_cols,
                  pl.BlockSpec((SC_CONV, tn), lambda n, i: (0, n))],
        out_specs=[pl.BlockSpec((tm, tn), lambda n, i: (i, n)),
                   pl.BlockSpec((None, SUBLANES, tn), lambda n, i: (i // tiles_per_batch, 0, n))],
        out_shape=[jax.ShapeDtypeStruct((grp.rows, W_SC), BF16),
                   jax.ShapeDtypeStruct((grp.batch, SUBLANES, W_SC), F32)],
        scratch_shapes=[pltpu.VMEM((tm, tn), F32)],
        compiler_params=_params(("arbitrary", "arbitrary")),
        name="sconv_prompt",
    )(h, *w_parts, conv_w)
    return g, tail[:, SUBLANES - (SC_CONV - 1):, :]


def _sconv_sample(grp, h, w_in, idx, conv_w, hist):
    tm, tn = grp.rows, TN_ODD_CAST
    n_blocks = W_SC // tn
    n_tail = (SC_CONV - 1) * grp.batch
    cols = pl.BlockSpec((D_MODEL, tn), lambda n: (0, n))
    w_shape = jax.ShapeDtypeStruct((D_MODEL, W_SC), BF16)
    g, tail, wb, wc, wx = pl.pallas_call(
        functools.partial(_sconv_sample_kernel, n_t=grp.seq),
        grid=(n_blocks,),
        in_specs=[pl.BlockSpec((tm, D_MODEL), lambda n: (0, 0)),
                  pl.BlockSpec((None, D_MODEL, tn), lambda n: (idx, 0, n)),
                  pl.BlockSpec((None, D_MODEL, tn), lambda n: (idx, 0, n_blocks + n)),
                  pl.BlockSpec((None, D_MODEL, tn), lambda n: (idx, 0, 2 * n_blocks + n)),
                  pl.BlockSpec((SC_CONV, tn), lambda n: (0, n)),
                  pl.BlockSpec((n_tail, tn), lambda n: (0, n))],
        out_specs=[pl.BlockSpec((tm, tn), lambda n: (0, n)), pl.BlockSpec((n_tail, tn), lambda n: (0, n)),
                   cols, cols, cols],
        out_shape=[jax.ShapeDtypeStruct((grp.rows, W_SC), BF16), jax.ShapeDtypeStruct((n_tail, W_SC), F32),
                   w_shape, w_shape, w_shape],
        compiler_params=_params(("arbitrary",)),
        name="sconv_sample",
    )(h, w_in, w_in, w_in, conv_w, hist)
    return g, tail, (wb, wc, wx)


def _rope_tables(pos):
    half = DH // 2
    inv = ROPE_THETA ** (-jnp.arange(half, dtype=F32) / half)
    ang = pos.astype(F32)[:, None] * inv[None, :]
    cos = jnp.cos(ang)
    sin = jnp.sin(ang)
    cos = jnp.concatenate([cos, cos, cos, cos], axis=-1)
    sin = jnp.concatenate([-sin, sin, -sin, sin], axis=-1)
    return cos, sin


def _block_diag(w):
    per = GATE_CHUNK // BLK_LRU
    w = w.reshape(W_LRU // GATE_CHUNK, per, BLK_LRU, BLK_LRU)
    eye = jnp.eye(per, dtype=w.dtype)
    out = jnp.einsum("cpij,pq->cpiqj", w, eye)
    return out.reshape(W_LRU // GATE_CHUNK, GATE_CHUNK, GATE_CHUNK).astype(BF16)


def _to_time_major(a):
    a = jnp.swapaxes(a, 0, 1)
    return a.reshape((a.shape[0] * a.shape[1],) + a.shape[2:])


def _from_time_major(a, batch):
    return jnp.swapaxes(a.reshape((a.shape[0] // batch, batch) + a.shape[1:]), 0, 1)


def _stack(parts):
    return parts[0][None] if len(parts) == 1 else jnp.stack(parts)


def _trunk(grp, x, mods, pos, state, kv_cache, W, cast):
    lru_h0, lru_hist, sconv_hist = state
    outs = {n: [] for n in ("k", "v", "h", "cl", "sc")}
    unrow = (lambda a: _from_time_major(a, grp.batch)) if grp.time_major else (
        lambda a: a.reshape((grp.batch, a.shape[0] // grp.batch) + a.shape[1:]))
    h = _first_norm(grp, x, W["norm_mix"][0], mods["mix"][0])
    for layer in range(DEPTH):
        if layer % 2 == 0:
            e = layer // 2
            lam_init = 0.8 - 0.6 * math.exp(-0.3 * layer)
            cos, sin = _rope_tables(pos)
            lam_vecs = [W[n][e][None, :] for n in ("lambda_q1", "lambda_k1", "lambda_q2", "lambda_k2")]
            subln = W["subln"][e][None, :]
            if grp.time_major:
                cos, sin = jnp.repeat(cos, grp.batch, axis=0), jnp.repeat(sin, grp.batch, axis=0)
                q, kf, vf, xg, cast["in_even", e] = _even_in_proj_sample(
                    grp, h, W["w_in_even"], e, W["q_norm"][e], W["k_norm"][e], cos, sin)
                cache_k, cache_v, page_table = kv_cache
                o = _attn_sample(grp, e, q, kf, vf, cache_k, cache_v, page_table, lam_vecs, subln, lam_init)
                yl, h_last = _lru_sample(grp, xg, _to_time_major(lru_hist[e]), lru_h0[e], W["lru"][e])
                conv_tail = unrow(xg[(grp.seq - (CONV_LRU - 1)) * grp.batch:, :W_LRU])
            else:
                yield from _await(cast, ("in_even", e))
                wq, wk, wv, wl = cast["in_even", e]
                q1, q2, kf, kb, vf, vb = _even_in_proj_prompt(grp, h, wq, wk, wv, W["q_norm"][e], W["k_norm"][e],
                                                              cos, sin)
                to_cast = () if "mlp" in cast else ((W["mlp_w1"], TF_MLP), (W["mlp_w2"], None),
                                                    (W["w_out_even"], None), (W["w_out_odd"], None))
                o, casted = _attn_prompt(grp, q1, q2, kb, vb, lam_vecs, subln, lam_init, to_cast)
                if to_cast:
                    cast["mlp"] = casted[:2]
                    cast["w_out_even"], cast["w_out_odd"] = casted[2:]
                yl, h_last, conv_tail = _lru_prompt(grp, h, wl, W["lru"][e])
            outs["k"].append(unrow(kf).reshape(grp.batch, grp.seq, H_ATT, 2 * DH))
            outs["v"].append(unrow(vf).reshape(grp.batch, grp.seq, H_ATT, DV))
            outs["h"].append(h_last)
            outs["cl"].append(conv_tail)
            acts, w_out = [o, yl], ("w_out_even", e)
        else:
            o_idx = layer // 2
            if grp.time_major:
                g, sc, cast["in_odd", o_idx] = _sconv_sample(grp, h, W["w_in_odd"], o_idx, W["sconv_w"][o_idx],
                                                             _to_time_major(sconv_hist[o_idx]))
                sc = unrow(sc)
            else:
                yield from _await(cast, ("in_odd", o_idx))
                g, sc = _sconv_prompt(grp, h, cast["in_odd", o_idx], W["sconv_w"][o_idx])
            outs["sc"].append(sc)
            acts, w_out = [g], ("w_out_odd", o_idx)
        yield from _await(cast, w_out[0])
        x, h = _out_proj(grp, acts, (cast[w_out[0]], w_out[1]), x, mods["mix"][layer], W["norm_mlp"][layer],
                         mods["mlp"][layer], "out_proj_%d" % layer)
        yield from _await(cast, "mlp")
        w1b, w2b = cast["mlp"]
        last = layer + 1 == DEPTH
        res = _mlp(grp, h, w1b, w2b, layer, x, mods["mlp"][layer], None if last else W["norm_mix"][layer + 1],
                   None if last else mods["mix"][layer + 1], "mlp_%d" % layer)
        x = res[0]
        h = None if last else res[1]
    return unrow(x), {k: _stack(v) for k, v in outs.items()}


def _await(cast, key):
    while key not in cast:
        yield key


def _run_trunks(trunks):
    results = {}
    while trunks:
        for name in list(trunks):
            try:
                next(trunks[name])
            except StopIteration as done:
                results[name] = done.value
                del trunks[name]
    return results


def kernel(x_prompt, x_sample, cache_k, cache_v, state_lru_h, state_lru_conv, state_sconv, page_table, c_prompt, c_sample, norm_mix, norm_mlp, ada_mix_w, ada_mix_b, ada_mlp_w, ada_mlp_b, mlp_w1, mlp_w2, w_in_even, w_out_even, lru_conv_w, lru_conv_b, lru_wa, lru_ba, lru_wx, lru_bx, lru_lam, q_norm, k_norm, lambda_q1, lambda_k1, lambda_q2, lambda_k2, subln, w_in_odd, sconv_w, w_out_odd):
    bsz, seq, _ = x_prompt.shape
    dec_b, dec_seq, _ = x_sample.shape
    past_len = page_table.shape[1] * cache_k.shape[2]
    prompt = _Group(bsz, seq, time_major=False)
    sample = _Group(dec_b, dec_seq, time_major=True)

    n_c = bsz + dec_b
    pad = (-n_c) % (2 * SUBLANES)
    c_all = jnp.pad(jnp.concatenate([c_sample, c_prompt], axis=0), ((0, pad), (0, 0)))
    mods_p, mods_s = {}, {}
    for kind, w, b in (("mix", ada_mix_w, ada_mix_b), ("mlp", ada_mlp_w, ada_mlp_b)):
        m = _ada(c_all, w, b)
        mods_s[kind] = [m[l, :dec_b] for l in range(DEPTH)]
        mods_p[kind] = [m[l, dec_b:n_c].reshape(bsz, 1, 3 * D_MODEL) for l in range(DEPTH)]

    row = lambda a: a.reshape(a.shape[0], 1, a.shape[-1])
    W = {
        "norm_mix": row(norm_mix), "norm_mlp": row(norm_mlp),
        "mlp_w1": mlp_w1.reshape(DEPTH * D_MODEL, D_FF), "mlp_w2": mlp_w2.reshape(DEPTH * D_FF, D_MODEL),
        "w_in_even": w_in_even, "w_in_odd": w_in_odd,
        "w_out_even": w_out_even.reshape(-1, D_MODEL), "w_out_odd": w_out_odd.reshape(-1, D_MODEL),
        "sconv_w": sconv_w,
        "q_norm": jnp.tile(q_norm, (1, 2))[:, None, :], "k_norm": jnp.tile(k_norm, (1, 2))[:, None, :],
        "lambda_q1": lambda_q1, "lambda_k1": lambda_k1, "lambda_q2": lambda_q2, "lambda_k2": lambda_k2,
        "subln": subln,
        "lru": [(lru_conv_w[e], lru_conv_b[e][None, :], _block_diag(lru_wa[e]), lru_ba[e].reshape(1, W_LRU),
                 _block_diag(lru_wx[e]), lru_bx[e].reshape(1, W_LRU), lru_lam[e][None, :])
                for e in range(N_EVEN)],
    }

    cast = {}
    done = _run_trunks({
        "sample": _trunk(sample, _to_time_major(x_sample), mods_s, past_len + jnp.arange(dec_seq),
                         (state_lru_h, state_lru_conv, state_sconv), (cache_k, cache_v, page_table), W, cast),
        "prompt": _trunk(prompt, x_prompt.reshape(prompt.rows, D_MODEL), mods_p, jnp.arange(seq),
                         (None, None, None), None, W, cast),
    })
    (y_s, o_s), (y_p, o_p) = done["sample"], done["prompt"]
    return (y_p, y_s, o_p["k"], o_p["v"], o_s["k"], o_s["v"], o_p["h"], o_s["h"],
            o_p["cl"], o_s["cl"], o_p["sc"], o_s["sc"])
```

```python
import functools
import math

import jax
import jax.numpy as jnp
from jax import lax
from jax.experimental import pallas as pl
from jax.experimental.pallas import tpu as pltpu

D_MODEL = 2048
DEPTH = 2
PAGE_SIZE = 128
N_EVEN = (DEPTH + 1) // 2
N_ODD = DEPTH // 2
W_LRU = D_MODEL // 2
H_LRU = 16
BLK_LRU = W_LRU // H_LRU
CONV_LRU = 4
LRU_C = 8.0
H_ATT = 8
DV = (D_MODEL // 2) // H_ATT
DH = DV // 2
W_ATT = H_ATT * DV
W_SC = D_MODEL
SC_CONV = 3
D_FF = 4 * D_MODEL
ROPE_THETA = 10000.0
RMS_EPS = 1e-6
NEG_INF = -1e30

F32 = jnp.float32
BF16 = jnp.bfloat16

SUBLANES = 8
LANES = 128
MIB = 1 << 20
VMEM_LIMIT = 56 * MIB

TN_ADA = 2048
TM_PROJ = 1024
TM_OUT = 512
TF_MLP = 1024
TN_ODD = 512
TN_ODD_CAST = 256
QK_COLS = 256
TQ_ATT = 512
HEADS_PER_STEP = 2
T_LRU = 512
GATE_CHUNK = 256
EPILOGUE_ROWS = 16


def _params(sem):
    return pltpu.CompilerParams(dimension_semantics=sem, vmem_limit_bytes=VMEM_LIMIT)


def _dot(a, b):
    return jnp.dot(a, b, preferred_element_type=F32)


def _dot_nt(a, b):
    return lax.dot_general(a, b, (((1,), (1,)), ((), ())), preferred_element_type=F32)


def _modnorm(x, g, shift, scale):
    ms = jnp.mean(x * x, axis=-1, keepdims=True)
    y = x * lax.rsqrt(ms + RMS_EPS)
    return (y * g) * (1.0 + scale) + shift


def _mod_slabs(n_rows, mod_rows):
    out = []
    for r in range(0, n_rows, EPILOGUE_ROWS):
        m = slice(None) if mod_rows == 1 else slice(r % mod_rows, r % mod_rows + EPILOGUE_ROWS)
        out.append((slice(r, r + EPILOGUE_ROWS), m))
    return out


def _modnorm_rows(x_of, g, shift_ref, scale_ref, mrows):
    parts = []
    for lo in range(0, EPILOGUE_ROWS, SUBLANES):
        if isinstance(mrows, slice) and mrows.start is not None:
            m = slice(mrows.start + lo, mrows.start + lo + SUBLANES)
        else:
            m = mrows
        parts.append(_modnorm(x_of(lo, lo + SUBLANES), g, shift_ref[m, :], scale_ref[m, :]))
    return jnp.concatenate(parts, axis=0).astype(BF16)


def _ada_kernel(c_ref, w_ref, b_ref, o_ref):
    c = c_ref[...]
    a = (c * jax.nn.sigmoid(c)).astype(BF16)
    o_ref[...] = _dot(a, w_ref[...].astype(BF16)) + b_ref[...]


def _ada(c_all, w, b):
    rows = c_all.shape[0]
    tn = TN_ADA
    return pl.pallas_call(
        _ada_kernel,
        grid=(DEPTH, 3 * D_MODEL // tn),
        in_specs=[
            pl.BlockSpec((rows, D_MODEL), lambda l, j: (0, 0)),
            pl.BlockSpec((None, D_MODEL, tn), lambda l, j: (l, 0, j)),
            pl.BlockSpec((None, 1, tn), lambda l, j: (l, 0, j)),
        ],
        out_specs=pl.BlockSpec((None, rows, tn), lambda l, j: (l, 0, j)),
        out_shape=jax.ShapeDtypeStruct((DEPTH, rows, 3 * D_MODEL), F32),
        compiler_params=_params(("arbitrary", "arbitrary")),
        name="ada_modulation",
    )(c_all, w, b.reshape(DEPTH, 1, 3 * D_MODEL))


class _Group:
    def __init__(self, batch, seq, time_major):
        self.batch = batch
        self.seq = seq
        self.rows = batch * seq
        self.time_major = time_major

    def mod_spec(self, tm, part, row_axis=0):
        if self.time_major:
            assert tm % self.batch == 0
            return pl.BlockSpec((self.batch, D_MODEL), lambda *g: (0, part))
        tiles_per_batch = self.seq // tm
        return pl.BlockSpec((None, 1, D_MODEL), lambda *g: (g[row_axis] // tiles_per_batch, 0, part))


def _modnorm_kernel(x_ref, g_ref, shift_ref, scale_ref, h_ref):
    g = g_ref[...]
    for rows, mrows in _mod_slabs(x_ref.shape[0], shift_ref.shape[0]):
        x_of = lambda lo, hi, r0=rows.start: x_ref[r0 + lo:r0 + hi, :]
        h_ref[rows, :] = _modnorm_rows(x_of, g, shift_ref, scale_ref, mrows)


def _first_norm(grp, x, g, mod):
    tm = TM_OUT
    return pl.pallas_call(
        _modnorm_kernel,
        grid=(grp.rows // tm,),
        in_specs=[
            pl.BlockSpec((tm, D_MODEL), lambda i: (i, 0)),
            pl.BlockSpec((1, D_MODEL), lambda i: (0, 0)),
            grp.mod_spec(tm, 0),
            grp.mod_spec(tm, 1),
        ],
        out_specs=pl.BlockSpec((tm, D_MODEL), lambda i: (i, 0)),
        out_shape=jax.ShapeDtypeStruct((grp.rows, D_MODEL), BF16),
        compiler_params=_params(("arbitrary",)),
        name="first_norm",
    )(x, g, mod, mod)


def _weight_cols(w_ref, wb_ref, cols):
    w = w_ref[:, cols]
    if wb_ref is None:
        return w
    w = w.astype(BF16)
    wb_ref[:, cols] = w
    return w


def _qk_heads(h_ref, w_ref, wb_ref, g_ref, cos_ref, sin_ref):
    h = h_ref[...]
    g, cos, sin = g_ref[...], cos_ref[...], sin_ref[...]
    tm = h.shape[0]
    lane = lax.broadcasted_iota(jnp.int32, (tm, LANES), 1)
    first_map = lane < DH
    first_half = (lane % DH) < (DH // 2)
    out = []
    for c in range(W_ATT // QK_COLS):
        acc = _dot(h, _weight_cols(w_ref, wb_ref, slice(c * QK_COLS, (c + 1) * QK_COLS)))
        for hd in range(QK_COLS // LANES):
            xs = acc[:, hd * LANES:(hd + 1) * LANES]
            sq = xs * xs
            lo = jnp.sum(jnp.where(first_map, sq, 0.0), axis=-1, keepdims=True)
            hi = jnp.sum(jnp.where(first_map, 0.0, sq), axis=-1, keepdims=True)
            ms = jnp.where(first_map, lo, hi) * (1.0 / DH)
            y = xs * lax.rsqrt(ms + RMS_EPS) * g
            rot = jnp.where(first_half,
                            pltpu.roll(y, LANES - DH // 2, axis=1),
                            pltpu.roll(y, DH // 2, axis=1))
            out.append(y * cos + rot * sin)
    return out, first_map


def _q_prompt_kernel(h_ref, w_ref, g_ref, cos_ref, sin_ref, q1_ref, q2_ref):
    slabs, first_map = _qk_heads(h_ref, w_ref, None, g_ref, cos_ref, sin_ref)
    for hd, y in enumerate(slabs):
        y = y * (DH ** -0.5)
        cols = slice(hd * LANES, (hd + 1) * LANES)
        q1_ref[:, cols] = jnp.where(first_map, y, 0.0).astype(BF16)
        q2_ref[:, cols] = jnp.where(first_map, 0.0, y).astype(BF16)


def _k_prompt_kernel(h_ref, w_ref, g_ref, cos_ref, sin_ref, kf_ref, kb_ref):
    slabs, _ = _qk_heads(h_ref, w_ref, None, g_ref, cos_ref, sin_ref)
    for hd, y in enumerate(slabs):
        cols = slice(hd * LANES, (hd + 1) * LANES)
        kf_ref[:, cols] = y
        kb_ref[:, cols] = y.astype(BF16)


def _qk_sample_kernel(h_ref, w_ref, g_ref, cos_ref, sin_ref, o_ref, wb_ref, *, scale):
    slabs, _ = _qk_heads(h_ref, w_ref, wb_ref, g_ref, cos_ref, sin_ref)
    for hd, y in enumerate(slabs):
        o_ref[:, hd * LANES:(hd + 1) * LANES] = y * scale


def _v_prompt_kernel(h_ref, w_ref, vf_ref, vb_ref):
    acc = _dot(h_ref[...], w_ref[...])
    vf_ref[...] = acc
    vb_ref[...] = acc.astype(BF16)


def _plain_sample_kernel(h_ref, w_ref, o_ref, wb_ref):
    o_ref[...] = _dot(h_ref[...], _weight_cols(w_ref, wb_ref, slice(None)))


def _proj_call(kernel, grp, h, w, n_col_blocks, extra, extra_specs, out_dtypes, name):
    tm = min(TM_PROJ, grp.rows)
    tn = W_ATT
    out_spec = pl.BlockSpec((tm, tn), lambda i, j: (i, j))
    out_specs = [out_spec] * len(out_dtypes)
    out_shape = [jax.ShapeDtypeStruct((grp.rows, tn * n_col_blocks), dt) for dt in out_dtypes]
    w_cols = pl.BlockSpec((D_MODEL, tn), lambda i, j: (0, j))
    if isinstance(w, tuple):
        w, idx, col_block = w
        w_spec = pl.BlockSpec((None, D_MODEL, tn), lambda i, j: (idx, 0, col_block + j))
        out_specs.append(w_cols)
        out_shape.append(jax.ShapeDtypeStruct((D_MODEL, tn * n_col_blocks), BF16))
    else:
        w_spec = w_cols
    return pl.pallas_call(
        kernel,
        grid=(grp.rows // tm, n_col_blocks),
        in_specs=[pl.BlockSpec((tm, D_MODEL), lambda i, j: (i, 0)), w_spec] + extra_specs,
        out_specs=out_specs,
        out_shape=out_shape,
        compiler_params=_params(("arbitrary", "arbitrary")),
        name=name,
    )(h, w, *extra)


def _rope_specs(grp, cos):
    tm = min(TM_PROJ, grp.rows)
    tiles = cos.shape[0] // tm
    rope = pl.BlockSpec((tm, LANES), lambda i, j: (i % tiles, 0))
    return [pl.BlockSpec((1, LANES), lambda i, j: (0, 0)), rope, rope]


def _even_in_proj_sample(grp, h, w, e, qg, kg, cos, sin):
    specs = _rope_specs(grp, cos)
    q_kernel = functools.partial(_qk_sample_kernel, scale=DH ** -0.5)
    k_kernel = functools.partial(_qk_sample_kernel, scale=1.0)
    q, wq = _proj_call(q_kernel, grp, h, (w, e, 0), 1, (qg, cos, sin), specs, (F32,), "q_proj")
    kf, wk = _proj_call(k_kernel, grp, h, (w, e, 1), 1, (kg, cos, sin), specs, (F32,), "k_proj")
    vf, wv = _proj_call(_plain_sample_kernel, grp, h, (w, e, 2), 1, (), [], (F32,), "v_proj")
    xg, wl = _proj_call(_plain_sample_kernel, grp, h, (w, e, 3), 2, (), [], (F32,), "lru_proj")
    return q, kf, vf, xg, (wq, wk, wv, wl)


def _even_in_proj_prompt(grp, h, wq, wk, wv, qg, kg, cos, sin):
    specs = _rope_specs(grp, cos)
    q1, q2 = _proj_call(_q_prompt_kernel, grp, h, wq, 1, (qg, cos, sin), specs, (BF16, BF16), "q_proj")
    kf, kb = _proj_call(_k_prompt_kernel, grp, h, wk, 1, (kg, cos, sin), specs, (F32, BF16), "k_proj")
    vf, vb = _proj_call(_v_prompt_kernel, grp, h, wv, 1, (), [], (F32, BF16), "v_proj")
    return q1, q2, kf, kb, vf, vb


def _diff_lambda(lq1_ref, lk1_ref, lq2_ref, lk2_ref, lam_init):
    a = jnp.sum(lq1_ref[...] * lk1_ref[...], axis=-1, keepdims=True)
    b = jnp.sum(lq2_ref[...] * lk2_ref[...], axis=-1, keepdims=True)
    return jnp.exp(a) - jnp.exp(b) + lam_init


def _sub_norm(o, g, lam_init):
    ms = jnp.mean(o * o, axis=-1, keepdims=True)
    return (o * lax.rsqrt(ms + RMS_EPS) * g) * (1.0 - lam_init)


def _attn_prompt_kernel(lq1_ref, lk1_ref, lq2_ref, lk2_ref, g_ref, q1_ref, q2_ref, k_ref, v_ref, *cast_refs,
                        tq, n_tiles, lam_init):
    n_cast = len(cast_refs) // 2
    o_ref = cast_refs[n_cast]
    qi = pl.program_id(2)
    lam = _diff_lambda(lq1_ref, lk1_ref, lq2_ref, lk2_ref, lam_init)
    row = lax.broadcasted_iota(jnp.int32, (tq, tq), 0)
    col = lax.broadcasted_iota(jnp.int32, (tq, tq), 1)
    causal = col <= row

    def body(c):
        for src_ref, dst_ref in zip(cast_refs[:n_cast], cast_refs[n_cast + 1:]):
            if len(dst_ref.shape) == 2:
                dst_ref[...] = src_ref[...].astype(BF16)
            else:
                tile = dst_ref.shape[2]
                for j in range(dst_ref.shape[0]):
                    dst_ref[j] = src_ref[:, j * tile:(j + 1) * tile].astype(BF16)
        n_past = c * tq
        for hd in range(o_ref.shape[1] // DV):
            cols = slice(hd * DV, (hd + 1) * DV)
            kd = k_ref[n_past:n_past + tq, cols]
            vd = v_ref[n_past:n_past + tq, cols]
            maps = []
            for q_ref in (q1_ref, q2_ref):
                q = q_ref[:, cols]
                sd = jnp.where(causal, _dot_nt(q, kd), NEG_INF)
                m = jnp.max(sd, axis=-1, keepdims=True)
                if c > 0:
                    sp = _dot_nt(q, k_ref[0:n_past, cols])
                    m = jnp.maximum(m, jnp.max(sp, axis=-1, keepdims=True))
                pd = jnp.exp(sd - m)
                l = jnp.sum(pd, axis=-1, keepdims=True)
                acc = _dot(pd.astype(BF16), vd)
                if c > 0:
                    pp = jnp.exp(sp - m)
                    l = l + jnp.sum(pp, axis=-1, keepdims=True)
                    acc = acc + _dot(pp.astype(BF16), v_ref[0:n_past, cols])
                maps.append(acc / l)
            o = maps[0] - lam * maps[1]
            o_ref[:, cols] = _sub_norm(o, g_ref[...], lam_init).astype(BF16)

    for c in range(n_tiles):
        pl.when(qi == c)(functools.partial(body, c))


def _attn_prompt(grp, q1, q2, kb, vb, lam_vecs, subln, lam_init, to_cast):
    tq = TQ_ATT
    tiles = grp.seq // tq
    n_groups = H_ATT // HEADS_PER_STEP
    width = HEADS_PER_STEP * DV
    steps = grp.batch * n_groups * tiles
    vec64 = pl.BlockSpec((1, DH), lambda b, h, i: (0, 0))
    q_spec = pl.BlockSpec((tq, width), lambda b, h, i: (b * tiles + i, h))
    kv_spec = pl.BlockSpec((grp.seq, width), lambda b, h, i: (b, h))
    step = lambda b, h, i: (b * n_groups + h) * tiles + i
    src_specs, dst_specs, dst_shapes = [], [], []
    for a, tile in to_cast:
        rows, cols = a.shape
        assert rows % (steps * 2 * SUBLANES) == 0
        src_specs.append(pl.BlockSpec((rows // steps, cols), lambda b, h, i: (step(b, h, i), 0)))
        if tile is None:
            dst_specs.append(src_specs[-1])
            dst_shapes.append(jax.ShapeDtypeStruct((rows, cols), BF16))
        else:
            dst_specs.append(pl.BlockSpec((cols // tile, rows // steps, tile), lambda b, h, i: (0, step(b, h, i), 0)))
            dst_shapes.append(jax.ShapeDtypeStruct((cols // tile, rows, tile), BF16))
    res = pl.pallas_call(
        functools.partial(_attn_prompt_kernel, tq=tq, n_tiles=tiles, lam_init=lam_init),
        grid=(grp.batch, n_groups, tiles),
        in_specs=[vec64] * 4 + [pl.BlockSpec((1, DV), lambda b, h, i: (0, 0)), q_spec, q_spec, kv_spec, kv_spec]
        + src_specs,
        out_specs=[q_spec] + dst_specs,
        out_shape=[jax.ShapeDtypeStruct((grp.rows, W_ATT), BF16)] + dst_shapes,
        compiler_params=_params(("arbitrary", "arbitrary", "arbitrary")),
        name="attn_prompt",
    )(*lam_vecs, subln, q1, q2, kb, vb, *[a for a, _ in to_cast])
    return res[0], res[1:]


def _attn_sample_kernel(pt_ref, lq1_ref, lk1_ref, lq2_ref, lk2_ref, g_ref, q_ref, kn_ref, vn_ref, *rest,
                        n_pages, lam_init):
    k_refs = rest[:n_pages]
    v_refs = rest[n_pages:2 * n_pages]
    o_ref = rest[2 * n_pages]
    del pt_ref
    n_q = q_ref.shape[0]
    half = H_ATT // 2
    per_head = 2 * n_q
    n_past = n_pages * PAGE_SIZE
    lam = _diff_lambda(lq1_ref, lk1_ref, lq2_ref, lk2_ref, lam_init)
    g = g_ref[...]
    lane = lax.broadcasted_iota(jnp.int32, (n_q, LANES), 1)
    row = lax.broadcasted_iota(jnp.int32, (2 * per_head, 1), 0)
    t_of_row = row % n_q
    second = row >= per_head
    col = lax.broadcasted_iota(jnp.int32, (2 * per_head, 2 * n_past), 1)
    own_head = (col % 2) == (lax.broadcasted_iota(jnp.int32, (2 * per_head, 2 * n_past), 0) // per_head)
    for hp in range(half):
        heads = (hp, hp + half)
        qbd = []
        for hd in heads:
            q = q_ref[:, hd, :]
            qbd += [jnp.where(lane < DH, q, 0.0), jnp.where(lane < DH, 0.0, q)]
        qbd = jnp.concatenate(qbd, axis=0)
        pair_rows = pl.ds(hp, 2 * PAGE_SIZE, stride=half)
        kp = jnp.concatenate([r[pair_rows, :].astype(BF16) for r in k_refs], axis=0)
        vp = jnp.concatenate([r[pair_rows, :].astype(BF16) for r in v_refs], axis=0)
        s = jnp.where(own_head, _dot_nt(qbd.astype(BF16), kp), NEG_INF)
        kn = [kn_ref[:, hd, :] for hd in heads]
        vn = [vn_ref[:, hd, :] for hd in heads]
        s_new = []
        for j in range(n_q):
            kj = jnp.where(second, kn[1][j:j + 1, :], kn[0][j:j + 1, :])
            s_new.append(jnp.where(t_of_row >= j, jnp.sum(qbd * kj, axis=-1, keepdims=True), NEG_INF))
        m = jnp.max(s, axis=-1, keepdims=True)
        for sj in s_new:
            m = jnp.maximum(m, sj)
        p = jnp.exp(s - m)
        l = jnp.sum(p, axis=-1, keepdims=True)
        acc = _dot(p.astype(BF16), vp)
        for j, sj in enumerate(s_new):
            pj = jnp.exp(sj - m)
            l = l + pj
            acc = acc + pj * jnp.where(second, vn[1][j:j + 1, :], vn[0][j:j + 1, :])
        o = acc / l
        for a, hd in enumerate(heads):
            oa = o[a * per_head:a * per_head + n_q] - lam * o[a * per_head + n_q:(a + 1) * per_head]
            o_ref[:, hd, :] = _sub_norm(oa, g, lam_init)


def _attn_sample(grp, e, q, kf, vf, cache_k, cache_v, page_table, lam_vecs, subln, lam_init):
    n_pages = page_table.shape[1]
    tok = (grp.seq, grp.batch, H_ATT, LANES)
    vec64 = pl.BlockSpec((1, DH), lambda b, pt: (0, 0))
    new_spec = pl.BlockSpec((grp.seq, None, H_ATT, LANES), lambda b, pt: (0, b, 0, 0))

    def page_spec(p):
        return pl.BlockSpec((None, None, PAGE_SIZE * H_ATT, LANES),
                            lambda b, pt: (e, pt[b * n_pages + p], 0, 0))

    pages = [page_spec(p) for p in range(n_pages)]
    page_rows = cache_k.shape[:2] + (PAGE_SIZE * H_ATT, LANES)
    cache_k = cache_k.reshape(page_rows)
    cache_v = cache_v.reshape(page_rows)
    out = pl.pallas_call(
        functools.partial(_attn_sample_kernel, n_pages=n_pages, lam_init=lam_init),
        grid_spec=pltpu.PrefetchScalarGridSpec(
            num_scalar_prefetch=1,
            grid=(grp.batch,),
            in_specs=[vec64] * 4 + [pl.BlockSpec((1, DV), lambda b, pt: (0, 0)), new_spec, new_spec, new_spec]
            + pages + pages,
            out_specs=new_spec,
        ),
        out_shape=jax.ShapeDtypeStruct(tok, F32),
        compiler_params=_params(("arbitrary",)),
        name="attn_sample",
    )(page_table.reshape(-1), *lam_vecs, subln, q.reshape(tok), kf.reshape(tok), vf.reshape(tok),
      *([cache_k] * n_pages), *([cache_v] * n_pages))
    return out.reshape(grp.rows, W_ATT)


def _shifted(u, hist, s):
    row = lax.broadcasted_iota(jnp.int32, u.shape, 0)
    return jnp.where(row < s, pltpu.roll(hist, s, axis=0), pltpu.roll(u, s, axis=0))


def _scan_rows(a, b, carry):
    r, w = a.shape
    groups = r // SUBLANES
    a = a.reshape(groups, SUBLANES, w)
    b = b.reshape(groups, SUBLANES, w)
    sub = lax.broadcasted_iota(jnp.int32, a.shape, 1)
    s = 1
    while s < SUBLANES:
        valid = sub >= s
        b = jnp.where(valid, a * pltpu.roll(b, s, axis=1) + b, b)
        a = jnp.where(valid, a * pltpu.roll(a, s, axis=1), a)
        s *= 2
    hs = []
    for grp_i in range(groups):
        hg = a[grp_i] * carry + b[grp_i]
        hs.append(hg)
        carry = hg[SUBLANES - 1:, :]
    return hs


def _lru_gates(xc, wa, ba, wx, bx, lam):
    xb = xc.astype(BF16)
    r_parts, i_parts = [], []
    for c, (wa_c, wx_c) in enumerate(zip(wa, wx)):
        cols = slice(c * GATE_CHUNK, (c + 1) * GATE_CHUNK)
        r_parts.append(_dot(xb[:, cols], wa_c))
        i_parts.append(_dot(xb[:, cols], wx_c))
    cat = (lambda parts: parts[0] if len(parts) == 1 else jnp.concatenate(parts, axis=1))
    r = jax.nn.sigmoid(cat(r_parts) + ba)
    i = jax.nn.sigmoid(cat(i_parts) + bx)
    log_a = (-LRU_C) * r * jax.nn.softplus(-lam)
    a = jnp.exp(log_a)
    drive = jnp.sqrt(1.0 - a * a) * (i * xc)
    return a, drive


def _lru_prompt_kernel(h_ref, wl_ref, cw_ref, cb_ref, wa_ref, ba_ref, wx_ref, bx_ref, lam_ref,
                       y_ref, hlast_ref, ctail_ref, hist_ref, hprev_ref):
    t = pl.program_id(1)
    rows = h_ref.shape[0]

    @pl.when(t == 0)
    def _():
        hist_ref[...] = jnp.zeros_like(hist_ref)
        hprev_ref[...] = jnp.zeros_like(hprev_ref)

    hin = h_ref[...]
    xl = _dot(hin, wl_ref[:, :W_LRU])
    gl = _dot(hin, wl_ref[:, W_LRU:])
    hist = hist_ref[...]
    xc = xl * cw_ref[CONV_LRU - 1:CONV_LRU, :] + cb_ref[...]
    for s in range(1, CONV_LRU):
        xc = xc + _shifted(xl, hist, s) * cw_ref[CONV_LRU - 1 - s:CONV_LRU - s, :]
    n_chunks = W_LRU // GATE_CHUNK
    a, drive = _lru_gates(xc, [wa_ref[c] for c in range(n_chunks)], ba_ref[...],
                          [wx_ref[c] for c in range(n_chunks)], bx_ref[...], lam_ref[...])
    hs = _scan_rows(a, drive, hprev_ref[SUBLANES - 1:, :])
    y_ref[...] = (jax.nn.gelu(gl) * jnp.concatenate(hs, axis=0)).astype(BF16)
    hist_ref[...] = xl
    hprev_ref[...] = hs[-1]
    hlast_ref[...] = hs[-1]
    ctail_ref[...] = xl[rows - SUBLANES:, :]


def _lru_prompt(grp, h, wl, lru_w):
    tt = T_LRU
    tiles = grp.seq // tt
    n_chunks = W_LRU // GATE_CHUNK

    def full(shape):
        return pl.BlockSpec(shape, lambda b, t: (0,) * len(shape))

    tail = pl.BlockSpec((None, SUBLANES, W_LRU), lambda b, t: (b, 0, 0))
    tail_shape = jax.ShapeDtypeStruct((grp.batch, SUBLANES, W_LRU), F32)
    y, hlast, ctail = pl.pallas_call(
        _lru_prompt_kernel,
        grid=(grp.batch, tiles),
        in_specs=[pl.BlockSpec((tt, D_MODEL), lambda b, t: (b * tiles + t, 0)),
                  full((D_MODEL, 2 * W_LRU)),
                  full((CONV_LRU, W_LRU)), full((1, W_LRU)), full((n_chunks, GATE_CHUNK, GATE_CHUNK)),
                  full((1, W_LRU)), full((n_chunks, GATE_CHUNK, GATE_CHUNK)), full((1, W_LRU)), full((1, W_LRU))],
        out_specs=[pl.BlockSpec((tt, W_LRU), lambda b, t: (b * tiles + t, 0)), tail, tail],
        out_shape=[jax.ShapeDtypeStruct((grp.rows, W_LRU), BF16), tail_shape, tail_shape],
        scratch_shapes=[pltpu.VMEM((tt, W_LRU), F32), pltpu.VMEM((SUBLANES, W_LRU), F32)],
        compiler_params=_params(("arbitrary", "arbitrary")),
        name="lru_prompt",
    )(h, wl, *lru_w)
    return y, hlast[:, SUBLANES - 1, :], ctail[:, SUBLANES - (CONV_LRU - 1):, :]


def _lru_sample_kernel(xl_ref, gl_ref, hist_ref, h0_ref, cw_ref, cb_ref, wa_ref, ba_ref, wx_ref, bx_ref, lam_ref,
                       y_ref, h_ref, *, n_t):
    nb = h0_ref.shape[0]

    def slab(t):
        return slice(t * nb, (t + 1) * nb)

    def u(t):
        return xl_ref[slab(t), :] if t >= 0 else hist_ref[slab(CONV_LRU - 1 + t), :]

    xc = []
    for t in range(n_t):
        acc = u(t) * cw_ref[CONV_LRU - 1:CONV_LRU, :] + cb_ref[...]
        for s in range(1, CONV_LRU):
            acc = acc + u(t - s) * cw_ref[CONV_LRU - 1 - s:CONV_LRU - s, :]
        xc.append(acc)
    xc = jnp.concatenate(xc, axis=0)
    a, drive = _lru_gates(xc, [wa_ref[0]], ba_ref[...], [wx_ref[0]], bx_ref[...], lam_ref[...])
    h = h0_ref[...]
    for t in range(n_t):
        h = a[slab(t), :] * h + drive[slab(t), :]
        y_ref[slab(t), :] = (jax.nn.gelu(gl_ref[slab(t), :]) * h).astype(BF16)
    h_ref[...] = h


def _lru_sample(grp, xg, hist, h0, lru_w):
    gc = GATE_CHUNK
    n_chunks = W_LRU // gc
    vec = pl.BlockSpec((1, gc), lambda c: (0, c))
    gate_w = pl.BlockSpec((1, gc, gc), lambda c: (c, 0, 0))
    return pl.pallas_call(
        functools.partial(_lru_sample_kernel, n_t=grp.seq),
        grid=(n_chunks,),
        in_specs=[pl.BlockSpec((grp.rows, gc), lambda c: (0, c)),
                  pl.BlockSpec((grp.rows, gc), lambda c: (0, n_chunks + c)),
                  pl.BlockSpec((hist.shape[0], gc), lambda c: (0, c)),
                  pl.BlockSpec((grp.batch, gc), lambda c: (0, c)),
                  pl.BlockSpec((CONV_LRU, gc), lambda c: (0, c)), vec, gate_w, vec, gate_w, vec, vec],
        out_specs=[pl.BlockSpec((grp.rows, gc), lambda c: (0, c)), pl.BlockSpec((grp.batch, gc), lambda c: (0, c))],
        out_shape=[jax.ShapeDtypeStruct((grp.rows, W_LRU), BF16), jax.ShapeDtypeStruct((grp.batch, W_LRU), F32)],
        compiler_params=_params(("arbitrary",)),
        name="lru_sample",
    )(xg, xg, hist, h0, *lru_w)


def _residual_epilogue(acc, x_ref, gate_ref, norm, x_out_ref, h_out_ref):
    g = None if norm is None else norm[0][...]
    for rows, mrows in _mod_slabs(x_ref.shape[0], gate_ref.shape[0]):
        r0 = rows.start

        def x_new_of(lo, hi, r0=r0, mrows=mrows):
            m = mrows if mrows.start is None else slice(mrows.start + lo, mrows.start + hi)
            x_new = x_ref[r0 + lo:r0 + hi, :] + gate_ref[m, :] * acc[r0 + lo:r0 + hi, :]
            x_out_ref[r0 + lo:r0 + hi, :] = x_new
            return x_new

        if h_out_ref is not None:
            h_out_ref[rows, :] = _modnorm_rows(x_new_of, g, norm[1], norm[2], mrows)
        else:
            for lo in range(0, EPILOGUE_ROWS, SUBLANES):
                x_new_of(lo, lo + SUBLANES)


def _out_proj_kernel(*refs, n_in):
    a_refs = refs[:n_in]
    w_ref, x_ref, gate_ref, g_ref, shift_ref, scale_ref, x_out_ref, h_out_ref = refs[n_in:]
    acc = None
    k0 = 0
    for a_ref in a_refs:
        k = a_ref.shape[1]
        part = _dot(a_ref[...].astype(BF16), w_ref[k0:k0 + k, :])
        acc = part if acc is None else acc + part
        k0 += k
    _residual_epilogue(acc, x_ref, gate_ref, (g_ref, shift_ref, scale_ref), x_out_ref, h_out_ref)


def _out_proj(grp, acts, w, x, gate_mod, g_next, next_mod, name):
    w, idx = w
    tm = TM_OUT
    row = pl.BlockSpec((tm, D_MODEL), lambda i: (i, 0))
    return pl.pallas_call(
        functools.partial(_out_proj_kernel, n_in=len(acts)),
        grid=(grp.rows // tm,),
        in_specs=[pl.BlockSpec((tm, a.shape[1]), lambda i: (i, 0)) for a in acts] + [
            pl.BlockSpec((D_MODEL, D_MODEL), lambda i: (idx, 0)),
            row,
            grp.mod_spec(tm, 2),
            pl.BlockSpec((1, D_MODEL), lambda i: (0, 0)),
            grp.mod_spec(tm, 0),
            grp.mod_spec(tm, 1),
        ],
        out_specs=[row, row],
        out_shape=[jax.ShapeDtypeStruct((grp.rows, D_MODEL), F32), jax.ShapeDtypeStruct((grp.rows, D_MODEL), BF16)],
        compiler_params=_params(("arbitrary",)),
        name=name,
    )(*acts, w, x, gate_mod, g_next, next_mod, next_mod)


def _mlp_kernel(*refs, with_norm):
    if with_norm:
        h_ref, w1_ref, w2_ref, x_ref, gate_ref, g_ref, shift_ref, scale_ref, x_out_ref, h_out_ref = refs
        norm = (g_ref, shift_ref, scale_ref)
    else:
        h_ref, w1_ref, w2_ref, x_ref, gate_ref, x_out_ref = refs
        norm, h_out_ref = None, None
    f = pl.program_id(1)

    @pl.when(f == 0)
    def _():
        x_out_ref[...] = jnp.zeros_like(x_out_ref)

    hid = jnp.maximum(_dot(h_ref[...], w1_ref[...]), 0.0)
    x_out_ref[...] += _dot((hid * hid).astype(BF16), w2_ref[...])

    @pl.when(f == pl.num_programs(1) - 1)
    def _():
        _residual_epilogue(x_out_ref, x_ref, gate_ref, norm, x_out_ref, h_out_ref)


def _mlp(grp, h, w1, w2, layer, x, gate_mod, g_next, next_mod, name):
    tf = TF_MLP
    with_norm = g_next is not None
    if grp.rows <= 2 * TM_OUT:
        tm = grp.rows
        row = pl.BlockSpec((tm, D_MODEL), lambda i, f: (i, 0), pipeline_mode=pl.Buffered(1))
    else:
        tm = TM_OUT
        row = pl.BlockSpec((tm, D_MODEL), lambda i, f: (i, 0))
    n_f = D_FF // tf
    in_specs = [row,
                pl.BlockSpec((None, D_MODEL, tf), lambda i, f: (f, layer, 0)),
                pl.BlockSpec((tf, D_MODEL), lambda i, f: (layer * n_f + f, 0)),
                row,
                grp.mod_spec(tm, 2)]
    args = [h, w1, w2, x, gate_mod]
    out_specs = [row]
    out_shape = [jax.ShapeDtypeStruct((grp.rows, D_MODEL), F32)]
    if with_norm:
        in_specs += [pl.BlockSpec((1, D_MODEL), lambda i, f: (0, 0)), grp.mod_spec(tm, 0), grp.mod_spec(tm, 1)]
        args += [g_next, next_mod, next_mod]
        out_specs.append(row)
        out_shape.append(jax.ShapeDtypeStruct((grp.rows, D_MODEL), BF16))
    return pl.pallas_call(
        functools.partial(_mlp_kernel, with_norm=with_norm),
        grid=(grp.rows // tm, D_FF // tf),
        in_specs=in_specs,
        out_specs=out_specs,
        out_shape=out_shape,
        compiler_params=_params(("arbitrary", "arbitrary")),
        name=name,
    )(*args)


def _sconv_prompt_kernel(h_ref, wb_ref, wc_ref, wx_ref, cw_ref, g_ref, tail_ref, hist_ref, *, tiles_per_batch):
    i = pl.program_id(1)
    rows = h_ref.shape[0]

    @pl.when(i % tiles_per_batch == 0)
    def _():
        hist_ref[...] = jnp.zeros_like(hist_ref)

    h = h_ref[...]
    u = _dot(h, wc_ref[...]) * _dot(h, wx_ref[...])
    hist = hist_ref[...]
    conv = u * cw_ref[SC_CONV - 1:SC_CONV, :]
    for s in range(1, SC_CONV):
        conv = conv + _shifted(u, hist, s) * cw_ref[SC_CONV - 1 - s:SC_CONV - s, :]
    g_ref[...] = (_dot(h, wb_ref[...]) * conv).astype(BF16)
    hist_ref[...] = u
    tail_ref[...] = u[rows - SUBLANES:, :]


def _sconv_sample_kernel(h_ref, wb_ref, wc_ref, wx_ref, cw_ref, hist_ref, g_ref, tail_ref,
                         wb_out_ref, wc_out_ref, wx_out_ref, *, n_t):
    h = h_ref[...]
    every = slice(None)
    u = _dot(h, _weight_cols(wc_ref, wc_out_ref, every)) * _dot(h, _weight_cols(wx_ref, wx_out_ref, every))
    b = _dot(h, _weight_cols(wb_ref, wb_out_ref, every))
    nb = h.shape[0] // n_t

    def slab(t):
        return slice(t * nb, (t + 1) * nb)

    def ut(t):
        return u[slab(t), :] if t >= 0 else hist_ref[slab(SC_CONV - 1 + t), :]

    for t in range(n_t):
        conv = ut(t) * cw_ref[SC_CONV - 1:SC_CONV, :]
        for s in range(1, SC_CONV):
            conv = conv + ut(t - s) * cw_ref[SC_CONV - 1 - s:SC_CONV - s, :]
        g_ref[slab(t), :] = (b[slab(t), :] * conv).astype(BF16)
    tail_ref[...] = u[(n_t - (SC_CONV - 1)) * nb:, :]


def _sconv_prompt(grp, h, w_parts, conv_w):
    tm, tn = TM_PROJ, TN_ODD
    tiles_per_batch = grp.seq // tm
    w_cols = pl.BlockSpec((D_MODEL, tn), lambda n, i: (0, n))
    g, tail = pl.pallas_call(
        functools.partial(_sconv_prompt_kernel, tiles_per_batch=tiles_per_batch),
        grid=(W_SC // tn, grp.rows // tm),
        in_specs=[pl.BlockSpec((tm, D_MODEL), lambda n, i: (i, 0)), w_cols, w_cols, w_cols,
                  pl.BlockSpec((SC_CONV, tn), lambda n, i: (0, n))],
        out_specs=[pl.BlockSpec((tm, tn), lambda n, i: (i, n)),
                   pl.BlockSpec((None, SUBLANES, tn), lambda n, i: (i // tiles_per_batch, 0, n))],
        out_shape=[jax.ShapeDtypeStruct((grp.rows, W_SC), BF16),
                   jax.ShapeDtypeStruct((grp.batch, SUBLANES, W_SC), F32)],
        scratch_shapes=[pltpu.VMEM((tm, tn), F32)],
        compiler_params=_params(("arbitrary", "arbitrary")),
        name="sconv_prompt",
    )(h, *w_parts, conv_w)
    return g, tail[:, SUBLANES - (SC_CONV - 1):, :]


def _sconv_sample(grp, h, w_in, idx, conv_w, hist):
    tm, tn = grp.rows, TN_ODD_CAST
    n_blocks = W_SC // tn
    n_tail = (SC_CONV - 1) * grp.batch
    cols = pl.BlockSpec((D_MODEL, tn), lambda n: (0, n))
    w_shape = jax.ShapeDtypeStruct((D_MODEL, W_SC), BF16)
    g, tail, wb, wc, wx = pl.pallas_call(
        functools.partial(_sconv_sample_kernel, n_t=grp.seq),
        grid=(n_blocks,),
        in_specs=[pl.BlockSpec((tm, D_MODEL), lambda n: (0, 0)),
                  pl.BlockSpec((None, D_MODEL, tn), lambda n: (idx, 0, n)),
                  pl.BlockSpec((None, D_MODEL, tn), lambda n: (idx, 0, n_blocks + n)),
                  pl.BlockSpec((None, D_MODEL, tn), lambda n: (idx, 0, 2 * n_blocks + n)),
                  pl.BlockSpec((SC_CONV, tn), lambda n: (0, n)),
                  pl.BlockSpec((n_tail, tn), lambda n: (0, n))],
        out_specs=[pl.BlockSpec((tm, tn), lambda n: (0, n)), pl.BlockSpec((n_tail, tn), lambda n: (0, n)),
                   cols, cols, cols],
        out_shape=[jax.ShapeDtypeStruct((grp.rows, W_SC), BF16), jax.ShapeDtypeStruct((n_tail, W_SC), F32),
                   w_shape, w_shape, w_shape],
        compiler_params=_params(("arbitrary",)),
        name="sconv_sample",
    )(h, w_in, w_in, w_in, conv_w, hist)
    return g, tail, (wb, wc, wx)


def _rope_tables(pos):
    half = DH // 2
    inv = ROPE_THETA ** (-jnp.arange(half, dtype=F32) / half)
    ang = pos.astype(F32)[:, None] * inv[None, :]
    cos = jnp.cos(ang)
    sin = jnp.sin(ang)
    cos = jnp.concatenate([cos, cos, cos, cos], axis=-1)
    sin = jnp.concatenate([-sin, sin, -sin, sin], axis=-1)
    return cos, sin


def _block_diag(w):
    per = GATE_CHUNK // BLK_LRU
    w = w.reshape(W_LRU // GATE_CHUNK, per, BLK_LRU, BLK_LRU)
    eye = jnp.eye(per, dtype=w.dtype)
    out = jnp.einsum("cpij,pq->cpiqj", w, eye)
    return out.reshape(W_LRU // GATE_CHUNK, GATE_CHUNK, GATE_CHUNK).astype(BF16)


def _to_time_major(a):
    a = jnp.swapaxes(a, 0, 1)
    return a.reshape((a.shape[0] * a.shape[1],) + a.shape[2:])


def _from_time_major(a, batch):
    return jnp.swapaxes(a.reshape((a.shape[0] // batch, batch) + a.shape[1:]), 0, 1)


def _stack(parts):
    return parts[0][None] if len(parts) == 1 else jnp.stack(parts)


def _trunk(grp, x, mods, pos, state, kv_cache, W, cast):
    lru_h0, lru_hist, sconv_hist = state
    outs = {n: [] for n in ("k", "v", "h", "cl", "sc")}
    unrow = (lambda a: _from_time_major(a, grp.batch)) if grp.time_major else (
        lambda a: a.reshape((grp.batch, a.shape[0] // grp.batch) + a.shape[1:]))
    h = _first_norm(grp, x, W["norm_mix"][0], mods["mix"][0])
    for layer in range(DEPTH):
        if layer % 2 == 0:
            e = layer // 2
            lam_init = 0.8 - 0.6 * math.exp(-0.3 * layer)
            cos, sin = _rope_tables(pos)
            lam_vecs = [W[n][e][None, :] for n in ("lambda_q1", "lambda_k1", "lambda_q2", "lambda_k2")]
            subln = W["subln"][e][None, :]
            if grp.time_major:
                cos, sin = jnp.repeat(cos, grp.batch, axis=0), jnp.repeat(sin, grp.batch, axis=0)
                q, kf, vf, xg, cast["in_even", e] = _even_in_proj_sample(
                    grp, h, W["w_in_even"], e, W["q_norm"][e], W["k_norm"][e], cos, sin)
                cache_k, cache_v, page_table = kv_cache
                o = _attn_sample(grp, e, q, kf, vf, cache_k, cache_v, page_table, lam_vecs, subln, lam_init)
                yl, h_last = _lru_sample(grp, xg, _to_time_major(lru_hist[e]), lru_h0[e], W["lru"][e])
                conv_tail = unrow(xg[(grp.seq - (CONV_LRU - 1)) * grp.batch:, :W_LRU])
            else:
                yield from _await(cast, ("in_even", e))
                wq, wk, wv, wl = cast["in_even", e]
                q1, q2, kf, kb, vf, vb = _even_in_proj_prompt(grp, h, wq, wk, wv, W["q_norm"][e], W["k_norm"][e],
                                                              cos, sin)
                to_cast = () if "mlp" in cast else ((W["mlp_w1"], TF_MLP), (W["mlp_w2"], None),
                                                    (W["w_out_even"], None), (W["w_out_odd"], None))
                o, casted = _attn_prompt(grp, q1, q2, kb, vb, lam_vecs, subln, lam_init, to_cast)
                if to_cast:
                    cast["mlp"] = casted[:2]
                    cast["w_out_even"], cast["w_out_odd"] = casted[2:]
                yl, h_last, conv_tail = _lru_prompt(grp, h, wl, W["lru"][e])
            outs["k"].append(unrow(kf).reshape(grp.batch, grp.seq, H_ATT, 2 * DH))
            outs["v"].append(unrow(vf).reshape(grp.batch, grp.seq, H_ATT, DV))
            outs["h"].append(h_last)
            outs["cl"].append(conv_tail)
            acts, w_out = [o, yl], ("w_out_even", e)
        else:
            o_idx = layer // 2
            if grp.time_major:
                g, sc, cast["in_odd", o_idx] = _sconv_sample(grp, h, W["w_in_odd"], o_idx, W["sconv_w"][o_idx],
                                                             _to_time_major(sconv_hist[o_idx]))
                sc = unrow(sc)
            else:
                yield from _await(cast, ("in_odd", o_idx))
                g, sc = _sconv_prompt(grp, h, cast["in_odd", o_idx], W["sconv_w"][o_idx])
            outs["sc"].append(sc)
            acts, w_out = [g], ("w_out_odd", o_idx)
        yield from _await(cast, w_out[0])
        x, h = _out_proj(grp, acts, (cast[w_out[0]], w_out[1]), x, mods["mix"][layer], W["norm_mlp"][layer],
                         mods["mlp"][layer], "out_proj_%d" % layer)
        yield from _await(cast, "mlp")
        w1b, w2b = cast["mlp"]
        last = layer + 1 == DEPTH
        res = _mlp(grp, h, w1b, w2b, layer, x, mods["mlp"][layer], None if last else W["norm_mix"][layer + 1],
                   None if last else mods["mix"][layer + 1], "mlp_%d" % layer)
        x = res[0]
        h = None if last else res[1]
    return unrow(x), {k: _stack(v) for k, v in outs.items()}


def _await(cast, key):
    while key not in cast:
        yield key


def _run_trunks(trunks):
    results = {}
    while trunks:
        for name in list(trunks):
            try:
                next(trunks[name])
            except StopIteration as done:
                results[name] = done.value
                del trunks[name]
    return results


def kernel(x_prompt, x_sample, cache_k, cache_v, state_lru_h, state_lru_conv, state_sconv, page_table, c_prompt, c_sample, norm_mix, norm_mlp, ada_mix_w, ada_mix_b, ada_mlp_w, ada_mlp_b, mlp_w1, mlp_w2, w_in_even, w_out_even, lru_conv_w, lru_conv_b, lru_wa, lru_ba, lru_wx, lru_bx, lru_lam, q_norm, k_norm, lambda_q1, lambda_k1, lambda_q2, lambda_k2, subln, w_in_odd, sconv_w, w_out_odd):
    bsz, seq, _ = x_prompt.shape
    dec_b, dec_seq, _ = x_sample.shape
    past_len = page_table.shape[1] * cache_k.shape[2]
    prompt = _Group(bsz, seq, time_major=False)
    sample = _Group(dec_b, dec_seq, time_major=True)

    n_c = bsz + dec_b
    pad = (-n_c) % (2 * SUBLANES)
    c_all = jnp.pad(jnp.concatenate([c_sample, c_prompt], axis=0), ((0, pad), (0, 0)))
    mods_p, mods_s = {}, {}
    for kind, w, b in (("mix", ada_mix_w, ada_mix_b), ("mlp", ada_mlp_w, ada_mlp_b)):
        m = _ada(c_all, w, b)
        mods_s[kind] = [m[l, :dec_b] for l in range(DEPTH)]
        mods_p[kind] = [m[l, dec_b:n_c].reshape(bsz, 1, 3 * D_MODEL) for l in range(DEPTH)]

    row = lambda a: a.reshape(a.shape[0], 1, a.shape[-1])
    W = {
        "norm_mix": row(norm_mix), "norm_mlp": row(norm_mlp),
        "mlp_w1": mlp_w1.reshape(DEPTH * D_MODEL, D_FF), "mlp_w2": mlp_w2.reshape(DEPTH * D_FF, D_MODEL),
        "w_in_even": w_in_even, "w_in_odd": w_in_odd,
        "w_out_even": w_out_even.reshape(-1, D_MODEL), "w_out_odd": w_out_odd.reshape(-1, D_MODEL),
        "sconv_w": sconv_w,
        "q_norm": jnp.tile(q_norm, (1, 2))[:, None, :], "k_norm": jnp.tile(k_norm, (1, 2))[:, None, :],
        "lambda_q1": lambda_q1, "lambda_k1": lambda_k1, "lambda_q2": lambda_q2, "lambda_k2": lambda_k2,
        "subln": subln,
        "lru": [(lru_conv_w[e], lru_conv_b[e][None, :], _block_diag(lru_wa[e]), lru_ba[e].reshape(1, W_LRU),
                 _block_diag(lru_wx[e]), lru_bx[e].reshape(1, W_LRU), lru_lam[e][None, :])
                for e in range(N_EVEN)],
    }

    cast = {}
    done = _run_trunks({
        "sample": _trunk(sample, _to_time_major(x_sample), mods_s, past_len + jnp.arange(dec_seq),
                         (state_lru_h, state_lru_conv, state_sconv), (cache_k, cache_v, page_table), W, cast),
        "prompt": _trunk(prompt, x_prompt.reshape(prompt.rows, D_MODEL), mods_p, jnp.arange(seq),
                         (None, None, None), None, W, cast),
    })
    (y_s, o_s), (y_p, o_p) = done["sample"], done["prompt"]
    return (y_p, y_s, o_p["k"], o_p["v"], o_s["k"], o_s["v"], o_p["h"], o_s["h"],
            o_p["cl"], o_s["cl"], o_p["sc"], o_s["sc"])
```
